```python
import math
import jax, jax.numpy as jnp
from jax import lax
import numpy as np

D_MODEL = 1024
BATCH = 16
SEQ = 256
DEPTH = 4
DEC_BATCH = 8
DEC_SEQ = 1024
PAST_LEN = 256

GRID_W = 64
N_HEADS = 16
HEAD_DIM = 64
KV_HEADS = 4
Q_PER_KV = N_HEADS // KV_HEADS
QKV_DIM = (N_HEADS + 2 * KV_HEADS) * HEAD_DIM
WINDOW = 128
BLOCK = 128
ROPE_BASE = 10000.0
ROT_AXIS_DIM = HEAD_DIM // 2
ATTN_SCALE = HEAD_DIM ** -0.5
HY_ORDER = 2
HY_DIRS = 2
FILTER_BANDS = 16
FILTER_EMB = 1 + 2 * FILTER_BANDS
FILTER_HIDDEN = 64
FILTER_FREQ = 1.0
DECAY_FAST_PCT = 0.3
DECAY_SLOW_PCT = 1.5
DECAY_TARGET = 1e-2
D_FF = 2816
N_MOD = 9
LN_EPS = 1e-5
N_ATTN_LAYERS = (DEPTH + 1) // 2
N_HYENA_LAYERS = DEPTH // 2
DEEPNORM_ALPHA = (2 * DEPTH) ** 0.25
DEEPNORM_BETA = (8 * DEPTH) ** -0.25
NEG_INF = -1e30

kernel_name = "hybrid_swa_hyena_macaron_diffusion_step"


def layer_norm(x, g, b):
    xf = x.astype(jnp.float32)
    mu = jnp.mean(xf, -1, keepdims=True)
    var = jnp.mean(jnp.square(xf - mu), -1, keepdims=True)
    return ((xf - mu) * lax.rsqrt(var + LN_EPS) * g + b).astype(x.dtype)


def modulate(x, shift, scale):
    return x * (1 + scale[:, None]) + shift[:, None]


def swiglu(h, w1, w2):
    g, u = jnp.split(h @ w1, 2, axis=-1)
    return (jax.nn.silu(g) * u) @ w2


def grid_positions(L):
    rows = L // GRID_W
    r = jnp.repeat(jnp.arange(rows), GRID_W).astype(jnp.float32)
    c = jnp.tile(jnp.arange(GRID_W), rows).astype(jnp.float32)
    return r, c


def rope_axis(x, pos):
    half = ROT_AXIS_DIM // 2
    inv = ROPE_BASE ** (-jnp.arange(half, dtype=jnp.float32) / half)
    ang = pos[:, None] * inv[None]
    cos = jnp.cos(ang)[None, :, None, :]
    sin = jnp.sin(ang)[None, :, None, :]
    xf = x.astype(jnp.float32)
    x1, x2 = xf[..., :half], xf[..., half:]
    return jnp.concatenate([x1 * cos - x2 * sin, x2 * cos + x1 * sin], -1).astype(x.dtype)


def rope_2d(x):
    r, c = grid_positions(x.shape[1])
    return jnp.concatenate([rope_axis(x[..., :ROT_AXIS_DIM], r), rope_axis(x[..., ROT_AXIS_DIM:], c)], -1)


def split_qkv(h, w_qkv):
    B, L, _ = h.shape
    qkv = h @ w_qkv
    nq, nk = N_HEADS * HEAD_DIM, KV_HEADS * HEAD_DIM
    q = qkv[..., :nq].reshape(B, L, N_HEADS, HEAD_DIM)
    k = qkv[..., nq:nq + nk].reshape(B, L, KV_HEADS, HEAD_DIM)
    v = qkv[..., nq + nk:].reshape(B, L, KV_HEADS, HEAD_DIM)
    return q, k, v


def sink_logits(sink, B, Lq):
    s = sink.astype(jnp.float32).reshape(KV_HEADS, Q_PER_KV)[None, :, :, None, None]
    return jnp.broadcast_to(s, (B, KV_HEADS, Q_PER_KV, Lq, 1))


def attn_context(h, w_qkv, w_o, sink):
    B, L, _ = h.shape
    q, k, v = split_qkv(h, w_qkv)

    def block(b):
        qb = lax.dynamic_slice_in_dim(q, b * BLOCK, BLOCK, axis=1).reshape(B, BLOCK, KV_HEADS, Q_PER_KV, HEAD_DIM)
        s = jnp.einsum('bqkgd,bckd->bkgqc', qb, k).astype(jnp.float32) * ATTN_SCALE
        p = jax.nn.softmax(jnp.concatenate([s, sink_logits(sink, B, BLOCK)], -1), axis=-1)[..., :L]
        o = jnp.einsum('bkgqc,bckd->bqkgd', p.astype(v.dtype), v)
        return o.reshape(B, BLOCK, N_HEADS * HEAD_DIM)

    o = lax.map(block, jnp.arange(L // BLOCK))
    o = jnp.transpose(o, (1, 0, 2, 3)).reshape(B, L, N_HEADS * HEAD_DIM)
    return o @ w_o, k, v


def attn_latent(h, k_ctx, v_ctx, w_qkv, w_o, sink):
    B, L, _ = h.shape
    Lc = k_ctx.shape[1]
    S = BLOCK + 2 * WINDOW
    q, k, v = split_qkv(h, w_qkv)
    q, k = rope_2d(q), rope_2d(k)
    pad = ((0, 0), (WINDOW, WINDOW), (0, 0), (0, 0))
    kp, vp = jnp.pad(k, pad), jnp.pad(v, pad)

    def block(b):
        start = b * BLOCK
        qb = lax.dynamic_slice_in_dim(q, start, BLOCK, axis=1).reshape(B, BLOCK, KV_HEADS, Q_PER_KV, HEAD_DIM)
        kb = lax.dynamic_slice_in_dim(kp, start, S, axis=1)
        vb = lax.dynamic_slice_in_dim(vp, start, S, axis=1)
        qi = start + jnp.arange(BLOCK)
        kj = start - WINDOW + jnp.arange(S)
        valid = (jnp.abs(qi[:, None] - kj[None, :]) <= WINDOW) & (kj >= 0)[None] & (kj < L)[None]
        s_loc = jnp.einsum('bqkgd,bskd->bkgqs', qb, kb).astype(jnp.float32) * ATTN_SCALE
        s_loc = jnp.where(valid, s_loc, NEG_INF)
        s_ctx = jnp.einsum('bqkgd,bckd->bkgqc', qb, k_ctx).astype(jnp.float32) * ATTN_SCALE
        p = jax.nn.softmax(jnp.concatenate([s_loc, s_ctx, sink_logits(sink, B, BLOCK)], -1), axis=-1)
        o = (jnp.einsum('bkgqs,bskd->bqkgd', p[..., :S].astype(vb.dtype), vb)
             + jnp.einsum('bkgqc,bckd->bqkgd', p[..., S:S + Lc].astype(v_ctx.dtype), v_ctx))
        return o.reshape(B, BLOCK, N_HEADS * HEAD_DIM)

    o = lax.map(block, jnp.arange(L // BLOCK))
    o = jnp.transpose(o, (1, 0, 2, 3)).reshape(B, L, N_HEADS * HEAD_DIM)
    return o @ w_o


def short_conv(u, w, b):
    up = jnp.pad(u, ((0, 0), (1, 1), (0, 0)))
    return up[:, :-2] * w[0] + up[:, 1:-1] * w[1] + up[:, 2:] * w[2] + b


def hyena_filter_spectrum(L, f_w1, f_b1, f_w2, f_b2, f_w3):
    t = jnp.arange(L, dtype=jnp.float32) / L
    bands = jnp.arange(1, FILTER_BANDS + 1, dtype=jnp.float32)
    ph = 2 * jnp.pi * t[:, None] * bands[None]
    feats = jnp.concatenate([t[:, None], jnp.sin(ph), jnp.cos(ph)], -1)
    a = jnp.sin(FILTER_FREQ * (feats @ f_w1.astype(jnp.float32) + f_b1.astype(jnp.float32)))
    a = jnp.sin(FILTER_FREQ * (a @ f_w2.astype(jnp.float32) + f_b2.astype(jnp.float32)))
    hf = (a @ f_w3.astype(jnp.float32)).reshape(L, HY_ORDER, HY_DIRS, D_MODEL)
    max_decay = math.log(DECAY_TARGET) / DECAY_FAST_PCT
    min_decay = math.log(DECAY_TARGET) / DECAY_SLOW_PCT
    deltas = jnp.abs(jnp.linspace(min_decay, max_decay, D_MODEL, dtype=jnp.float32))
    hf = hf * jnp.exp(-t[:, None] * deltas[None])[:, None, None, :]
    fwd, bwd = hf[:, :, 0], hf[:, :, 1]
    full = jnp.concatenate([fwd, jnp.zeros((1, HY_ORDER, D_MODEL), jnp.float32), bwd[:0:-1]], axis=0)
    full = full / (jnp.sum(jnp.abs(full), axis=0, keepdims=True) + 1e-6)
    return jnp.fft.rfft(full, axis=0)


def long_conv(z, spec, d):
    L = z.shape[1]
    zf32 = z.astype(jnp.float32)
    zf = jnp.fft.rfft(zf32, n=2 * L, axis=1)
    y = jnp.fft.irfft(zf * spec[None], n=2 * L, axis=1)[:, :L]
    return (y + zf32 * d.astype(jnp.float32)).astype(z.dtype)


def hyena(h, w_in, conv_w, conv_b, f_w1, f_b1, f_w2, f_b2, f_w3, hy_d, w_out):
    L = h.shape[1]
    u = short_conv(h @ w_in, conv_w, conv_b)
    v, x1, x2 = jnp.split(u, 3, axis=-1)
    spec = hyena_filter_spectrum(L, f_w1, f_b1, f_w2, f_b2, f_w3)
    z = x1 * long_conv(v, spec[:, 0], hy_d[0])
    z = x2 * long_conv(z, spec[:, 1], hy_d[1])
    return z @ w_out


def trunk(x, cond, P, ctx_k=None, ctx_v=None):
    new_k, new_v = [], []
    for i in range(DEPTH):
        mod = (jax.nn.silu(cond) @ P['ada_w'][i] + P['ada_b'][i]).reshape(cond.shape[0], N_MOD, D_MODEL)
        h = modulate(x, mod[:, 0], mod[:, 1])
        f = swiglu(h, P['ffn_w1'][i, 0], P['ffn_w2'][i, 0])
        x = layer_norm(DEEPNORM_ALPHA * x + 0.5 * mod[:, 2][:, None] * f, P['ln_g'][i, 0], P['ln_b'][i, 0])
        h = modulate(x, mod[:, 3], mod[:, 4])
        j = i // 2
        if i % 2 == 0:
            if ctx_k is None:
                out, k, v = attn_context(h, P['attn_w_qkv'][j], P['attn_w_o'][j], P['attn_sink'][j])
                new_k.append(k)
                new_v.append(v)
            else:
                out = attn_latent(h, ctx_k[:, j], ctx_v[:, j], P['attn_w_qkv'][j], P['attn_w_o'][j], P['attn_sink'][j])
        else:
            out = hyena(h, P['hy_w_in'][j], P['hy_conv_w'][j], P['hy_conv_b'][j], P['hy_f_w1'][j], P['hy_f_b1'][j],
                        P['hy_f_w2'][j], P['hy_f_b2'][j], P['hy_f_w3'][j], P['hy_d'][j], P['hy_w_out'][j])
        x = layer_norm(DEEPNORM_ALPHA * x + mod[:, 5][:, None] * out, P['ln_g'][i, 1], P['ln_b'][i, 1])
        h = modulate(x, mod[:, 6], mod[:, 7])
        f = swiglu(h, P['ffn_w1'][i, 1], P['ffn_w2'][i, 1])
        x = layer_norm(DEEPNORM_ALPHA * x + 0.5 * mod[:, 8][:, None] * f, P['ln_g'][i, 2], P['ln_b'][i, 2])
    return x, new_k, new_v


def setup_inputs(seed: int = 0) -> dict:
    key = jax.random.key(seed)
    ks = jax.random.split(key, 26)
    f32 = jnp.float32
    nrm = lambda k, shape, s: jax.random.normal(k, shape, f32) * s
    D, A, Hy = D_MODEL, N_ATTN_LAYERS, N_HYENA_LAYERS
    return {
        "x_prompt": nrm(ks[0], (BATCH, SEQ, D), 1.0),
        "x_sample": nrm(ks[1], (DEC_BATCH, DEC_SEQ, D), 1.0),
        "cache_k": nrm(ks[2], (DEC_BATCH, A, PAST_LEN, KV_HEADS, HEAD_DIM), 1.0),
        "cache_v": nrm(ks[3], (DEC_BATCH, A, PAST_LEN, KV_HEADS, HEAD_DIM), 1.0),
        "c": nrm(ks[4], (DEC_BATCH, D), 1.0),
        "c_ctx": nrm(ks[5], (D,), 1.0),
        "ada_w": nrm(ks[6], (DEPTH, D, N_MOD * D), D ** -0.5),
        "ada_b": nrm(ks[7], (DEPTH, N_MOD * D), 0.02),
        "ln_g": 1.0 + nrm(ks[8], (DEPTH, 3, D), 0.02),
        "ln_b": nrm(ks[9], (DEPTH, 3, D), 0.02),
        "ffn_w1": nrm(ks[10], (DEPTH, 2, D, 2 * D_FF), D ** -0.5),
        "ffn_w2": nrm(ks[11], (DEPTH, 2, D_FF, D), D_FF ** -0.5 * DEEPNORM_BETA),
        "attn_w_qkv": nrm(ks[12], (A, D, QKV_DIM), D ** -0.5),
        "attn_w_o": nrm(ks[13], (A, N_HEADS * HEAD_DIM, D), (N_HEADS * HEAD_DIM) ** -0.5 * DEEPNORM_BETA),
        "attn_sink": nrm(ks[14], (A, N_HEADS), 0.5),
        "hy_w_in": nrm(ks[15], (Hy, D, 3 * D), D ** -0.5),
        "hy_conv_w": nrm(ks[16], (Hy, 3, 3 * D), 3 ** -0.5),
        "hy_conv_b": nrm(ks[17], (Hy, 3 * D), 0.02),
        "hy_f_w1": nrm(ks[18], (Hy, FILTER_EMB, FILTER_HIDDEN), FILTER_EMB ** -0.5 * 2.0),
        "hy_f_b1": nrm(ks[19], (Hy, FILTER_HIDDEN), 0.5),
        "hy_f_w2": nrm(ks[20], (Hy, FILTER_HIDDEN, FILTER_HIDDEN), FILTER_HIDDEN ** -0.5 * 2.0),
        "hy_f_b2": nrm(ks[21], (Hy, FILTER_HIDDEN), 0.5),
        "hy_f_w3": nrm(ks[22], (Hy, FILTER_HIDDEN, HY_ORDER * HY_DIRS * D), FILTER_HIDDEN ** -0.5),
        "hy_d": nrm(ks[23], (Hy, HY_ORDER, D), 0.1),
        "hy_w_out": nrm(ks[24], (Hy, D, D), D ** -0.5 * DEEPNORM_BETA),
    }


def reference(x_prompt, x_sample, cache_k, cache_v, c, c_ctx, ada_w, ada_b, ln_g, ln_b, ffn_w1, ffn_w2,
              attn_w_qkv, attn_w_o, attn_sink, hy_w_in, hy_conv_w, hy_conv_b, hy_f_w1, hy_f_b1, hy_f_w2,
              hy_f_b2, hy_f_w3, hy_d, hy_w_out):
    P = {
        'ada_w': ada_w, 'ada_b': ada_b, 'ln_g': ln_g, 'ln_b': ln_b,
        'ffn_w1': ffn_w1, 'ffn_w2': ffn_w2,
        'attn_w_qkv': attn_w_qkv, 'attn_w_o': attn_w_o, 'attn_sink': attn_sink,
        'hy_w_in': hy_w_in, 'hy_conv_w': hy_conv_w, 'hy_conv_b': hy_conv_b,
        'hy_f_w1': hy_f_w1, 'hy_f_b1': hy_f_b1, 'hy_f_w2': hy_f_w2, 'hy_f_b2': hy_f_b2,
        'hy_f_w3': hy_f_w3, 'hy_d': hy_d, 'hy_w_out': hy_w_out,
    }
    y_prompt, ks_list, vs_list = trunk(x_prompt, c_ctx[None], P)
    new_cache_k = jnp.stack(ks_list, axis=1)
    new_cache_v = jnp.stack(vs_list, axis=1)
    y_sample, _, _ = trunk(x_sample, c, P, cache_k, cache_v)
    return (y_prompt, y_sample, new_cache_k, new_cache_v)
```

```python
import functools
import math

import numpy as np
import jax
import jax.numpy as jnp
from jax import lax
from jax.experimental import pallas as pl
from jax.experimental.pallas import tpu as pltpu

D = 1024
BATCH, SEQ = 16, 256
DEC_BATCH, DEC_SEQ = 8, 1024
DEPTH = 4
N_HEADS, HEAD_DIM, KV_HEADS = 16, 64, 4
Q_PER_KV = N_HEADS // KV_HEADS
KV_DIM = KV_HEADS * HEAD_DIM
QKV_DIM = (N_HEADS + 2 * KV_HEADS) * HEAD_DIM
GRID_W = 64
WINDOW = 128
BLOCK = 128
WIN_KEYS = BLOCK + 2 * WINDOW
ROPE_BASE = 10000.0
ATTN_SCALE = HEAD_DIM ** -0.5
D_FF = 2816
N_MOD = 9
LN_EPS = 1e-5
ALPHA = (2 * DEPTH) ** 0.25
NEG_INF = -1e30
FILTER_BANDS = 16
FILTER_EMB = 1 + 2 * FILTER_BANDS
FILTER_HIDDEN = 64
DECAY_FAST_PCT, DECAY_SLOW_PCT, DECAY_TARGET = 0.3, 1.5, 1e-2

T_CTX = BATCH * SEQ
T_LAT = DEC_BATCH * DEC_SEQ
T_ALL = T_CTX + T_LAT
N_COND = 16
TM = 1024
FF_CHUNK = 256
LANES = 128
VMEM_LIMIT = 56 * 1024 * 1024

f32 = jnp.float32
bf16 = jnp.bfloat16


def _cond_of_tile(i):
    return jnp.where(i < T_CTX // TM, 0, 1 + (i * TM - T_CTX) // DEC_SEQ)


def _params(*sem):
    return pltpu.CompilerParams(dimension_semantics=sem, vmem_limit_bytes=VMEM_LIMIT)


def _layer_norm(y, g, b):
    mu = jnp.mean(y, axis=-1, keepdims=True)
    yc = y - mu
    var = jnp.mean(yc * yc, axis=-1, keepdims=True)
    return yc * lax.rsqrt(var + LN_EPS) * g + b


def _ada_kernel(c_ref, w_ref, b_ref, o_ref):
    s = jax.nn.silu(c_ref[...]).astype(bf16)
    o_ref[...] = jnp.dot(s, w_ref[...].astype(bf16), preferred_element_type=f32) + b_ref[...]


def _ada(cond, ada_w, ada_b):
    tn = 2304
    nt = (N_MOD * D) // tn
    out = pl.pallas_call(
        _ada_kernel,
        grid=(DEPTH, nt),
        in_specs=[
            pl.BlockSpec((N_COND, D), lambda l, n: (0, 0)),
            pl.BlockSpec((None, D, tn), lambda l, n: (l, 0, n)),
            pl.BlockSpec((None, 1, tn), lambda l, n: (l, 0, n)),
        ],
        out_specs=pl.BlockSpec((N_COND, tn), lambda l, n: (0, l * nt + n)),
        out_shape=jax.ShapeDtypeStruct((N_COND, DEPTH * N_MOD * D), f32),
        compiler_params=_params("arbitrary", "arbitrary"),
        name="ada_mod",
    )(cond, ada_w, ada_b.reshape(DEPTH, 1, N_MOD * D))
    return out.reshape(N_COND, DEPTH, N_MOD, D)


def _ffn_kernel(x_ref, mod_ref, w1g_ref, w1u_ref, w2_ref, g_ref, b_ref, o_ref, h_scr, acc_scr, *, mod_base):
    k = pl.program_id(1)

    @pl.when(k == 0)
    def _():
        shift = mod_ref[mod_base:mod_base + 1, :]
        scale = mod_ref[mod_base + 1:mod_base + 2, :]
        h_scr[...] = (x_ref[...] * (1.0 + scale) + shift).astype(bf16)
        acc_scr[...] = jnp.zeros_like(acc_scr)

    h = h_scr[...]
    g = jnp.dot(h, w1g_ref[...], preferred_element_type=f32)
    u = jnp.dot(h, w1u_ref[...], preferred_element_type=f32)
    a = (jax.nn.silu(g) * u).astype(bf16)
    acc_scr[...] += jnp.dot(a, w2_ref[...], preferred_element_type=f32)

    @pl.when(k == pl.num_programs(1) - 1)
    def _():
        gate = mod_ref[mod_base + 2:mod_base + 3, :]
        y = ALPHA * x_ref[...] + (0.5 * gate) * acc_scr[...]
        o_ref[...] = _layer_norm(y, g_ref[...], b_ref[...])


def _ffn(x, mod, w1, w2, ln_g, ln_b, layer, half):
    nk = D_FF // FF_CHUNK
    return pl.pallas_call(
        functools.partial(_ffn_kernel, mod_base=6 * half),
        grid=(T_ALL // TM, nk),
        in_specs=[
            pl.BlockSpec((TM, D), lambda i, k: (i, 0)),
            pl.BlockSpec((None, None, N_MOD, D), lambda i, k: (_cond_of_tile(i), layer, 0, 0)),
            pl.BlockSpec((None, None, D, FF_CHUNK), lambda i, k: (layer, half, 0, k)),
            pl.BlockSpec((None, None, D, FF_CHUNK), lambda i, k: (layer, half, 0, k + nk)),
            pl.BlockSpec((None, None, FF_CHUNK, D), lambda i, k: (layer, half, k, 0)),
            pl.BlockSpec((None, 1, D), lambda i, k: (3 * layer + 2 * half, 0, 0)),
            pl.BlockSpec((None, 1, D), lambda i, k: (3 * layer + 2 * half, 0, 0)),
        ],
        out_specs=pl.BlockSpec((TM, D), lambda i, k: (i, 0)),
        out_shape=jax.ShapeDtypeStruct((T_ALL, D), f32),
        scratch_shapes=[pltpu.VMEM((TM, D), bf16), pltpu.VMEM((TM, D), f32)],
        compiler_params=_params("parallel", "arbitrary"),
        name=f"ffn_l{layer}h{half}",
    )(x, mod, w1, w1, w2, ln_g, ln_b)


def _modmm_kernel(x_ref, mod_ref, w_ref, o_ref):
    shift = mod_ref[3:4, :]
    scale = mod_ref[4:5, :]
    h = (x_ref[...] * (1.0 + scale) + shift).astype(bf16)
    o_ref[...] = jnp.dot(h, w_ref[...], preferred_element_type=f32)


def _modmm(x, mod, w, layer, j, name):
    n = w.shape[-1]
    tn = 1536
    return pl.pallas_call(
        _modmm_kernel,
        grid=(T_ALL // TM, n // tn),
        in_specs=[
            pl.BlockSpec((TM, D), lambda i, c: (i, 0)),
            pl.BlockSpec((None, None, N_MOD, D), lambda i, c: (_cond_of_tile(i), layer, 0, 0)),
            pl.BlockSpec((None, D, tn), lambda i, c: (j, 0, c)),
        ],
        out_specs=pl.BlockSpec((TM, tn), lambda i, c: (i, c)),
        out_shape=jax.ShapeDtypeStruct((T_ALL, n), f32),
        compiler_params=_params("parallel", "arbitrary"),
        name=name,
    )(x, mod, w)


def _proj_kernel(x_ref, z_ref, mod_ref, w_ref, g_ref, b_ref, o_ref):
    gate = mod_ref[5:6, :]
    f = jnp.dot(z_ref[...].astype(bf16), w_ref[...], preferred_element_type=f32)
    y = ALPHA * x_ref[...] + gate * f
    o_ref[...] = _layer_norm(y, g_ref[...], b_ref[...])


def _proj(x, z, mod, w, ln_g, ln_b, layer, j, name):
    return pl.pallas_call(
        _proj_kernel,
        grid=(T_ALL // TM,),
        in_specs=[
            pl.BlockSpec((TM, D), lambda i: (i, 0)),
            pl.BlockSpec((TM, D), lambda i: (i, 0)),
            pl.BlockSpec((None, None, N_MOD, D), lambda i: (_cond_of_tile(i), layer, 0, 0)),
            pl.BlockSpec((None, D, D), lambda i: (j, 0, 0)),
            pl.BlockSpec((None, 1, D), lambda i: (3 * layer + 1, 0, 0)),
            pl.BlockSpec((None, 1, D), lambda i: (3 * layer + 1, 0, 0)),
        ],
        out_specs=pl.BlockSpec((TM, D), lambda i: (i, 0)),
        out_shape=jax.ShapeDtypeStruct((T_ALL, D), f32),
        compiler_params=_params("parallel"),
        name=name,
    )(x, z, mod, w, ln_g, ln_b)


def _softmax_pv(parts, sink):
    m = sink
    for s, _ in parts:
        m = jnp.maximum(m, jnp.max(s, axis=-1, keepdims=True))
    denom = jnp.exp(sink - m)
    o = None
    for s, v in parts:
        p = jnp.exp(s - m)
        denom = denom + jnp.sum(p, axis=-1, keepdims=True)
        pv = jnp.dot(p.astype(bf16), v, preferred_element_type=f32)
        o = pv if o is None else o + pv
    return o / denom


def _qk(q, k):
    return lax.dot_general(q, k, (((1,), (1,)), ((), ())), preferred_element_type=f32)


def _attn_ctx_kernel(sink_ref, q_ref, k_ref, v_ref, o_ref):
    q = (q_ref[...] * ATTN_SCALE).astype(bf16)
    k = k_ref[...].astype(bf16)
    v = v_ref[...].astype(bf16)
    for h in range(N_HEADS):
        g = h // Q_PER_KV
        qh = q[:, h * HEAD_DIM:(h + 1) * HEAD_DIM]
        kg = k[:, g * HEAD_DIM:(g + 1) * HEAD_DIM]
        vg = v[:, g * HEAD_DIM:(g + 1) * HEAD_DIM]
        o_ref[:, h * HEAD_DIM:(h + 1) * HEAD_DIM] = _softmax_pv([(_qk(qh, kg), vg)], sink_ref[h])


def _attn_ctx(qkv, sink):
    kcol = (N_HEADS * HEAD_DIM) // KV_DIM
    return pl.pallas_call(
        _attn_ctx_kernel,
        grid=(BATCH,),
        in_specs=[
            pl.BlockSpec(memory_space=pltpu.SMEM),
            pl.BlockSpec((SEQ, D), lambda b: (b, 0)),
            pl.BlockSpec((SEQ, KV_DIM), lambda b: (b, kcol)),
            pl.BlockSpec((SEQ, KV_DIM), lambda b: (b, kcol + 1)),
        ],
        out_specs=pl.BlockSpec((SEQ, D), lambda b: (b, 0)),
        out_shape=jax.ShapeDtypeStruct((T_ALL, D), f32),
        compiler_params=_params("parallel"),
        name="attn_ctx",
    )(sink, qkv, qkv, qkv)


def _rope(x, cos, sin):
    lane = lax.broadcasted_iota(jnp.int32, (x.shape[0], LANES), 1)
    first = (lane % (HEAD_DIM // 2)) < (HEAD_DIM // 4)
    cols = []
    for c in range(x.shape[1] // LANES):
        xc = x[:, c * LANES:(c + 1) * LANES]
        partner = jnp.where(first, pltpu.roll(xc, LANES - HEAD_DIM // 4, 1), pltpu.roll(xc, HEAD_DIM // 4, 1))
        cols.append(xc * cos + partner * sin)
    return jnp.concatenate(cols, axis=1)


def _attn_lat_kernel(sink_ref, q_ref, k_ref, v_ref, kc_ref, vc_ref, cq_ref, sq_ref, ck_ref, sk_ref, o_in_ref,
                     o_ref, kr_scr, vb_scr):
    del o_in_ref
    qb = pl.program_id(1)

    @pl.when(qb == 0)
    def _():
        kr_scr[...] = _rope(k_ref[...], ck_ref[...], sk_ref[...]).astype(bf16)
        vb_scr[...] = v_ref[...].astype(bf16)

    ws = pl.multiple_of(jnp.clip(qb * BLOCK - WINDOW, 0, DEC_SEQ - WIN_KEYS), BLOCK)
    q = (_rope(q_ref[...], cq_ref[...], sq_ref[...]) * ATTN_SCALE).astype(bf16)
    kw = kr_scr[pl.ds(ws, WIN_KEYS), :]
    vw = vb_scr[pl.ds(ws, WIN_KEYS), :]
    kc = kc_ref[...].astype(bf16)
    vc = vc_ref[...].astype(bf16)
    qi = qb * BLOCK + lax.broadcasted_iota(jnp.int32, (BLOCK, WIN_KEYS), 0)
    kj = ws + lax.broadcasted_iota(jnp.int32, (BLOCK, WIN_KEYS), 1)
    valid = jnp.abs(qi - kj) <= WINDOW
    for h in range(N_HEADS):
        g = h // Q_PER_KV
        hs = slice(h * HEAD_DIM, (h + 1) * HEAD_DIM)
        gs = slice(g * HEAD_DIM, (g + 1) * HEAD_DIM)
        qh = q[:, hs]
        s_loc = jnp.where(valid, _qk(qh, kw[:, gs]), NEG_INF)
        s_ctx = _qk(qh, kc[:, gs])
        o_ref[:, hs] = _softmax_pv([(s_loc, vw[:, gs]), (s_ctx, vc[:, gs])], sink_ref[h])


def _attn_lat(qkv, o_ctx, cache_k, cache_v, sink, j, rope_cos, rope_sin):
    kcol = (N_HEADS * HEAD_DIM) // KV_DIM
    row0 = T_CTX // DEC_SEQ
    nqb = DEC_SEQ // BLOCK
    return pl.pallas_call(
        _attn_lat_kernel,
        grid=(DEC_BATCH, nqb),
        in_specs=[
            pl.BlockSpec(memory_space=pltpu.SMEM),
            pl.BlockSpec((BLOCK, D), lambda b, i: ((row0 + b) * nqb + i, 0)),
            pl.BlockSpec((DEC_SEQ, KV_DIM), lambda b, i: (row0 + b, kcol)),
            pl.BlockSpec((DEC_SEQ, KV_DIM), lambda b, i: (row0 + b, kcol + 1)),
            pl.BlockSpec((None, None, SEQ, KV_DIM), lambda b, i: (b, j, 0, 0)),
            pl.BlockSpec((None, None, SEQ, KV_DIM), lambda b, i: (b, j, 0, 0)),
            pl.BlockSpec((BLOCK, LANES), lambda b, i: (i, 0)),
            pl.BlockSpec((BLOCK, LANES), lambda b, i: (i, 0)),
            pl.BlockSpec((DEC_SEQ, LANES), lambda b, i: (0, 0)),
            pl.BlockSpec((DEC_SEQ, LANES), lambda b, i: (0, 0)),
            pl.BlockSpec(memory_space=pl.ANY),
        ],
        out_specs=pl.BlockSpec((BLOCK, D), lambda b, i: ((row0 + b) * nqb + i, 0)),
        out_shape=jax.ShapeDtypeStruct((T_ALL, D), f32),
        scratch_shapes=[pltpu.VMEM((DEC_SEQ, KV_DIM), bf16), pltpu.VMEM((DEC_SEQ, KV_DIM), bf16)],
        input_output_aliases={10: 0},
        compiler_params=_params("parallel", "arbitrary"),
        name="attn_lat",
    )(sink, qkv, qkv, qkv, cache_k, cache_v, rope_cos, rope_sin, rope_cos, rope_sin, o_ctx)


def _rope_tables():
    t = np.arange(DEC_SEQ)
    pos = np.stack([t // GRID_W, t % GRID_W], axis=1).astype(np.float64)
    lane = np.arange(LANES)
    hl = lane % HEAD_DIM
    axis = hl // (HEAD_DIM // 2)
    half = HEAD_DIM // 4
    inv = ROPE_BASE ** (-(hl % half).astype(np.float64) / half)
    ang = pos[:, axis] * inv[None, :]
    first = (hl % (HEAD_DIM // 2)) < half
    cos = np.cos(ang)
    sin = np.where(first[None, :], -np.sin(ang), np.sin(ang))
    return jnp.asarray(cos, f32), jnp.asarray(sin, f32)


def _dft_tables(L):
    idx = np.arange(L)
    ft = np.outer(idx, idx) % (2 * L)
    cm = np.cos(np.pi * ft / L)
    sm = np.sin(np.pi * ft / L)
    nyq = np.where(idx % 2 == 0, 1.0, -1.0)
    sm_n = sm.copy()
    sm_n[0, :] = nyq
    fwd = np.concatenate([cm, sm_n], axis=0)
    wc = np.full((L,), 2.0)
    wc[0] = 1.0
    inv_c = cm.T * wc[None, :]
    inv_s = sm.T * 2.0
    inv_s[:, 0] = nyq
    inv = np.concatenate([inv_c, inv_s], axis=1) / (2 * L)
    return cm, sm, nyq, fwd, inv


def _filter_tables(L):
    t = np.arange(L, dtype=np.float64) / L
    bands = np.arange(1, FILTER_BANDS + 1, dtype=np.float64)
    ph = 2 * np.pi * t[:, None] * bands[None]
    feats = np.zeros((L, LANES))
    feats[:, :FILTER_EMB] = np.concatenate([t[:, None], np.sin(ph), np.cos(ph)], -1)
    max_decay = math.log(DECAY_TARGET) / DECAY_FAST_PCT
    min_decay = math.log(DECAY_TARGET) / DECAY_SLOW_PCT
    deltas = np.abs(np.linspace(min_decay, max_decay, D))
    decay = np.exp(-t[:, None] * deltas[None])
    return feats, decay


def _dot_hi(a, b):
    return jnp.dot(a, b, preferred_element_type=f32, precision=lax.Precision.HIGHEST)


def _filter_kernel(feat_ref, w1_ref, b1_ref, w2_ref, b2_ref, w3f_ref, w3b_ref, dec_ref, cm_ref, sm_ref, nyq_ref,
                   o_ref):
    a = jnp.sin(_dot_hi(feat_ref[...], w1_ref[...]) + b1_ref[...])
    a = jnp.sin(_dot_hi(a, w2_ref[...]) + b2_ref[...])
    dec = dec_ref[...]
    fwd = _dot_hi(a, w3f_ref[...]) * dec
    bwd = _dot_hi(a, w3b_ref[...]) * dec
    row = lax.broadcasted_iota(jnp.int32, fwd.shape, 0)
    bwd = jnp.where(row == 0, 0.0, bwd)
    norm = jnp.sum(jnp.abs(fwd), axis=0, keepdims=True) + jnp.sum(jnp.abs(bwd), axis=0, keepdims=True) + 1e-6
    even = (fwd + bwd) / norm
    odd = (bwd - fwd) / norm
    hr = _dot_hi(cm_ref[...], even)
    hi = _dot_hi(sm_ref[...], odd)
    h_nyq = jnp.sum(even * nyq_ref[...], axis=0, keepdims=True)
    o_ref[0] = hr
    o_ref[1] = jnp.where(row == 0, h_nyq, hi)


def _hyena_filter(L, j, w1p, b1p, w2p, b2p, w3p, name):
    cm, sm, nyq, _, _ = _dft_tables(L)
    feats, decay = _filter_tables(L)
    td = 512
    nd = D // td
    full = lambda shape: pl.BlockSpec(shape, lambda o, d: (0,) * len(shape))
    return pl.pallas_call(
        _filter_kernel,
        grid=(2, nd),
        in_specs=[
            full((L, LANES)),
            pl.BlockSpec((None, LANES, LANES), lambda o, d: (j, 0, 0)),
            pl.BlockSpec((None, 1, LANES), lambda o, d: (j, 0, 0)),
            pl.BlockSpec((None, LANES, LANES), lambda o, d: (j, 0, 0)),
            pl.BlockSpec((None, 1, LANES), lambda o, d: (j, 0, 0)),
            pl.BlockSpec((None, LANES, td), lambda o, d: (j, 0, (2 * o) * nd + d)),
            pl.BlockSpec((None, LANES, td), lambda o, d: (j, 0, (2 * o + 1) * nd + d)),
            pl.BlockSpec((L, td), lambda o, d: (0, d)),
            full((L, L)),
            full((L, L)),
            full((L, 1)),
        ],
        out_specs=pl.BlockSpec((2, L, td), lambda o, d: (o, 0, d)),
        out_shape=jax.ShapeDtypeStruct((4, L, D), f32),
        compiler_params=_params("parallel", "parallel"),
        name=name,
    )(jnp.asarray(feats, f32), w1p, b1p, w2p, b2p, w3p, w3p, jnp.asarray(decay, f32),
      jnp.asarray(cm, f32), jnp.asarray(sm, f32), jnp.asarray(nyq[:, None], f32))


def _short_conv(u, w, b):
    L = u.shape[0]
    row = lax.broadcasted_iota(jnp.int32, u.shape, 0)
    prev = jnp.where(row == 0, 0.0, pltpu.roll(u, 1, 0))
    nxt = jnp.where(row == L - 1, 0.0, pltpu.roll(u, L - 1, 0))
    return prev * w[0:1, :] + u * w[1:2, :] + nxt * w[2:3, :] + b


def _long_conv(z, hr, hi, fwd, inv):
    L = z.shape[0]
    zs = jnp.dot(fwd, z.astype(bf16), preferred_element_type=f32)
    zr, zi = zs[:L], zs[L:]
    row = lax.broadcasted_iota(jnp.int32, hr.shape, 0)
    hi_m = jnp.where(row == 0, 0.0, hi)
    yr = zr * hr + zi * hi_m
    yi = zi * jnp.where(row == 0, hi, hr) - zr * hi_m
    ys = jnp.concatenate([yr, yi], axis=0).astype(bf16)
    return jnp.dot(inv, ys, preferred_element_type=f32)


def _hyena_conv_kernel(pv_ref, p1_ref, p2_ref, wv_ref, w1_ref, w2_ref, bv_ref, b1_ref, b2_ref, h_ref, d_ref,
                       fwd_ref, inv_ref, o_in_ref, o_ref):
    del o_in_ref
    v = _short_conv(pv_ref[...], wv_ref[...], bv_ref[...])
    x1 = _short_conv(p1_ref[...], w1_ref[...], b1_ref[...])
    x2 = _short_conv(p2_ref[...], w2_ref[...], b2_ref[...])
    fwd, inv = fwd_ref[...], inv_ref[...]
    z = x1 * (_long_conv(v, h_ref[0], h_ref[1], fwd, inv) + v * d_ref[0:1, :])
    z = x2 * (_long_conv(z, h_ref[2], h_ref[3], fwd, inv) + z * d_ref[1:2, :])
    o_ref[...] = z


def _hyena_conv(p, z_in, conv_w, conv_b, spec, hy_d, j, L, nb, row0, td, name):
    _, _, _, fwd, inv = _dft_tables(L)
    nd = D // td
    in_specs = []
    for c in range(3):
        in_specs.append(pl.BlockSpec((L, td), lambda d, b, c=c: (row0 + b, c * nd + d)))
    for c in range(3):
        in_specs.append(pl.BlockSpec((None, 3, td), lambda d, b, c=c: (j, 0, c * nd + d)))
    for c in range(3):
        in_specs.append(pl.BlockSpec((None, 1, td), lambda d, b, c=c: (j, 0, c * nd + d)))
    in_specs += [
        pl.BlockSpec((4, L, td), lambda d, b: (0, 0, d)),
        pl.BlockSpec((None, 2, td), lambda d, b: (j, 0, d)),
        pl.BlockSpec((2 * L, L), lambda d, b: (0, 0)),
        pl.BlockSpec((L, 2 * L), lambda d, b: (0, 0)),
    ]
    args = [p, p, p, conv_w, conv_w, conv_w, conv_b, conv_b, conv_b, spec, hy_d,
            jnp.asarray(fwd, bf16), jnp.asarray(inv, bf16)]
    aliases = {}
    if z_in is not None:
        in_specs.append(pl.BlockSpec(memory_space=pl.ANY))
        args.append(z_in)
        aliases = {len(args) - 1: 0}
        kernel = _hyena_conv_kernel
    else:
        kernel = lambda *refs: _hyena_conv_kernel(*refs[:-1], None, refs[-1])
    return pl.pallas_call(
        kernel,
        grid=(nd, nb),
        in_specs=in_specs,
        out_specs=pl.BlockSpec((L, td), lambda d, b: (row0 + b, d)),
        out_shape=jax.ShapeDtypeStruct((T_ALL, D), f32),
        input_output_aliases=aliases,
        compiler_params=_params("parallel", "parallel"),
        name=name,
    )(*args)


def _pad2(a, rows, cols):
    return jnp.pad(a, [(0, 0)] * (a.ndim - 2) + [(0, rows - a.shape[-2]), (0, cols - a.shape[-1])])


def kernel(x_prompt, x_sample, cache_k, cache_v, c, c_ctx, ada_w, ada_b, ln_g, ln_b, ffn_w1, ffn_w2, attn_w_qkv,
           attn_w_o, attn_sink, hy_w_in, hy_conv_w, hy_conv_b, hy_f_w1, hy_f_b1, hy_f_w2, hy_f_b2, hy_f_w3, hy_d,
           hy_w_out):
    x = jnp.concatenate([x_prompt.reshape(T_CTX, D), x_sample.reshape(T_LAT, D)], axis=0)
    cond = jnp.concatenate([c_ctx[None], c, jnp.zeros((N_COND - 1 - DEC_BATCH, D), f32)], axis=0)
    mod = _ada(cond, ada_w, ada_b)

    w1 = ffn_w1.astype(bf16)
    w2 = ffn_w2.astype(bf16)
    w_qkv = attn_w_qkv.astype(bf16)
    w_o = attn_w_o.astype(bf16)
    w_in = hy_w_in.astype(bf16)
    w_out = hy_w_out.astype(bf16)
    ck = cache_k.reshape(DEC_BATCH, -1, SEQ, KV_DIM)
    cv = cache_v.reshape(DEC_BATCH, -1, SEQ, KV_DIM)
    rope_cos, rope_sin = _rope_tables()
    fw1 = _pad2(hy_f_w1, LANES, LANES)
    fb1 = _pad2(hy_f_b1[:, None, :], 1, LANES)
    fw2 = _pad2(hy_f_w2, LANES, LANES)
    fb2 = _pad2(hy_f_b2[:, None, :], 1, LANES)
    fw3 = _pad2(hy_f_w3, LANES, hy_f_w3.shape[-1])
    conv_b = hy_conv_b[:, None, :]
    ln_g = ln_g.reshape(DEPTH * 3, 1, D)
    ln_b = ln_b.reshape(DEPTH * 3, 1, D)

    new_k, new_v = [], []
    for i in range(DEPTH):
        j = i // 2
        x = _ffn(x, mod, w1, w2, ln_g, ln_b, i, 0)
        if i % 2 == 0:
            qkv = _modmm(x, mod, w_qkv, i, j, f"qkv_l{i}")
            kv_ctx = qkv[:T_CTX, N_HEADS * HEAD_DIM:]
            new_k.append(kv_ctx[:, :KV_DIM].reshape(BATCH, SEQ, KV_HEADS, HEAD_DIM))
            new_v.append(kv_ctx[:, KV_DIM:].reshape(BATCH, SEQ, KV_HEADS, HEAD_DIM))
            sink = attn_sink[j]
            z = _attn_ctx(qkv, sink)
            z = _attn_lat(qkv, z, ck, cv, sink, j, rope_cos, rope_sin)
            x = _proj(x, z, mod, w_o, ln_g, ln_b, i, j, f"attn_out_l{i}")
        else:
            p = _modmm(x, mod, w_in, i, j, f"hy_in_l{i}")
            spec_ctx = _hyena_filter(SEQ, j, fw1, fb1, fw2, fb2, fw3, f"hy_filter_ctx_l{i}")
            spec_lat = _hyena_filter(DEC_SEQ, j, fw1, fb1, fw2, fb2, fw3, f"hy_filter_lat_l{i}")
            z = _hyena_conv(p, None, hy_conv_w, conv_b, spec_ctx, hy_d, j, SEQ, BATCH, 0, 512, f"hy_conv_ctx_l{i}")
            z = _hyena_conv(p, z, hy_conv_w, conv_b, spec_lat, hy_d, j, DEC_SEQ, DEC_BATCH, T_CTX // DEC_SEQ, 256,
                            f"hy_conv_lat_l{i}")
            x = _proj(x, z, mod, w_out, ln_g, ln_b, i, j, f"hy_out_l{i}")
        x = _ffn(x, mod, w1, w2, ln_g, ln_b, i, 1)

    y_prompt = x[:T_CTX].reshape(BATCH, SEQ, D)
    y_sample = x[T_CTX:].reshape(DEC_BATCH, DEC_SEQ, D)
    return y_prompt, y_sample, jnp.stack(new_k, axis=1), jnp.stack(new_v, axis=1)
```

```python
import functools
import math

import numpy as np
import jax
import jax.numpy as jnp
from jax import lax
from jax.experimental import pallas as pl
from jax.experimental.pallas import tpu as pltpu

D = 1024
BATCH, SEQ = 16, 256
DEC_BATCH, DEC_SEQ = 8, 1024
DEPTH = 4
N_HEADS, HEAD_DIM, KV_HEADS = 16, 64, 4
Q_PER_KV = N_HEADS // KV_HEADS
Q_DIM = N_HEADS * HEAD_DIM
KV_DIM = KV_HEADS * HEAD_DIM
GRID_W = 64
WINDOW = 128
BLOCK = 128
WIN_KEYS = BLOCK + 2 * WINDOW
ROPE_BASE = 10000.0
ATTN_SCALE = HEAD_DIM ** -0.5
D_FF = 2816
N_MOD = 9
LN_EPS = 1e-5
ALPHA = (2 * DEPTH) ** 0.25
NEG_INF = -1e30
FILTER_BANDS = 16
FILTER_EMB = 1 + 2 * FILTER_BANDS
DECAY_FAST_PCT, DECAY_SLOW_PCT, DECAY_TARGET = 0.3, 1.5, 1e-2

T_CTX = BATCH * SEQ
T_LAT = DEC_BATCH * DEC_SEQ
T_ALL = T_CTX + T_LAT
N_COND = 16
TM = 1024
FF_CHUNK = 256
LANES = 128
VMEM_LIMIT = 56 * 1024 * 1024

f32 = jnp.float32
bf16 = jnp.bfloat16


def _cond_of_tile(i):
    return jnp.where(i < T_CTX // TM, 0, 1 + (i * TM - T_CTX) // DEC_SEQ)


def _params(*sem):
    return pltpu.CompilerParams(dimension_semantics=sem, vmem_limit_bytes=VMEM_LIMIT)


def _layer_norm(y, g, b):
    mu = jnp.mean(y, axis=-1, keepdims=True)
    yc = y - mu
    var = jnp.mean(yc * yc, axis=-1, keepdims=True)
    return yc * lax.rsqrt(var + LN_EPS) * g + b


def _modulate(x_ref, mod_ref, base):
    shift = mod_ref[base:base + 1, :]
    scale = mod_ref[base + 1:base + 2, :]
    return (x_ref[...] * (1.0 + scale) + shift).astype(bf16)


def _dot_nt(a, b):
    return lax.dot_general(a, b, (((1,), (1,)), ((), ())), preferred_element_type=f32)


def _ada_kernel(c_ref, w_ref, b_ref, o_ref):
    s = jax.nn.silu(c_ref[...]).astype(bf16)
    o_ref[...] = jnp.dot(s, w_ref[...].astype(bf16), preferred_element_type=f32) + b_ref[...]


def _ada(cond, ada_w, ada_b):
    tn = 2304
    nt = (N_MOD * D) // tn
    out = pl.pallas_call(
        _ada_kernel,
        grid=(DEPTH, nt),
        in_specs=[
            pl.BlockSpec((N_COND, D), lambda l, n: (0, 0)),
            pl.BlockSpec((None, D, tn), lambda l, n: (l, 0, n)),
            pl.BlockSpec((None, 1, tn), lambda l, n: (l, 0, n)),
        ],
        out_specs=pl.BlockSpec((N_COND, tn), lambda l, n: (0, l * nt + n)),
        out_shape=jax.ShapeDtypeStruct((N_COND, DEPTH * N_MOD * D), f32),
        compiler_params=_params("arbitrary", "arbitrary"),
        name="ada_mod",
    )(cond, ada_w, ada_b.reshape(DEPTH, 1, N_MOD * D))
    return out.reshape(N_COND, DEPTH, N_MOD, D)


def _ffn_kernel(x_ref, mod_ref, w1g_ref, w1u_ref, w2_ref, g_ref, b_ref, o_ref, h_scr, acc_scr, *, mod_base):
    k = pl.program_id(1)

    @pl.when(k == 0)
    def _():
        h_scr[...] = _modulate(x_ref, mod_ref, mod_base)
        acc_scr[...] = jnp.zeros_like(acc_scr)

    h = h_scr[...]
    g = jnp.dot(h, w1g_ref[...], preferred_element_type=f32)
    u = jnp.dot(h, w1u_ref[...], preferred_element_type=f32)
    a = (jax.nn.silu(g) * u).astype(bf16)
    acc_scr[...] += jnp.dot(a, w2_ref[...], preferred_element_type=f32)

    @pl.when(k == pl.num_programs(1) - 1)
    def _():
        gate = mod_ref[mod_base + 2:mod_base + 3, :]
        y = ALPHA * x_ref[...] + (0.5 * gate) * acc_scr[...]
        o_ref[...] = _layer_norm(y, g_ref[...], b_ref[...])


def _ffn(x, mod, w1, w2, ln_g, ln_b, layer, half):
    nk = D_FF // FF_CHUNK
    return pl.pallas_call(
        functools.partial(_ffn_kernel, mod_base=6 * half),
        grid=(T_ALL // TM, nk),
        in_specs=[
            pl.BlockSpec((TM, D), lambda i, k: (i, 0)),
            pl.BlockSpec((None, None, N_MOD, D), lambda i, k: (_cond_of_tile(i), layer, 0, 0)),
            pl.BlockSpec((None, None, D, FF_CHUNK), lambda i, k: (layer, half, 0, k)),
            pl.BlockSpec((None, None, D, FF_CHUNK), lambda i, k: (layer, half, 0, k + nk)),
            pl.BlockSpec((None, None, FF_CHUNK, D), lambda i, k: (layer, half, k, 0)),
            pl.BlockSpec((None, 1, D), lambda i, k: (3 * layer + 2 * half, 0, 0)),
            pl.BlockSpec((None, 1, D), lambda i, k: (3 * layer + 2 * half, 0, 0)),
        ],
        out_specs=pl.BlockSpec((TM, D), lambda i, k: (i, 0)),
        out_shape=jax.ShapeDtypeStruct((T_ALL, D), f32),
        scratch_shapes=[pltpu.VMEM((TM, D), bf16), pltpu.VMEM((TM, D), f32)],
        compiler_params=_params("parallel", "arbitrary"),
        name=f"ffn_l{layer}h{half}",
    )(x, mod, w1, w1, w2, ln_g, ln_b)


def _modmm_kernel(x_ref, mod_ref, w_ref, o_ref):
    o_ref[...] = jnp.dot(_modulate(x_ref, mod_ref, 3), w_ref[...], preferred_element_type=f32)


def _modmm(x, mod, w, layer, j, name):
    n = w.shape[-1]
    tn = 1536
    return pl.pallas_call(
        _modmm_kernel,
        grid=(T_ALL // TM, n // tn),
        in_specs=[
            pl.BlockSpec((TM, D), lambda i, c: (i, 0)),
            pl.BlockSpec((None, None, N_MOD, D), lambda i, c: (_cond_of_tile(i), layer, 0, 0)),
            pl.BlockSpec((None, D, tn), lambda i, c: (j, 0, c)),
        ],
        out_specs=pl.BlockSpec((TM, tn), lambda i, c: (i, c)),
        out_shape=jax.ShapeDtypeStruct((T_ALL, n), f32),
        compiler_params=_params("parallel", "arbitrary"),
        name=name,
    )(x, mod, w)


def _proj_kernel(x_ref, z_ref, mod_ref, w_ref, g_ref, b_ref, o_ref):
    gate = mod_ref[5:6, :]
    f = jnp.dot(z_ref[...], w_ref[...], preferred_element_type=f32)
    y = ALPHA * x_ref[...] + gate * f
    o_ref[...] = _layer_norm(y, g_ref[...], b_ref[...])


def _proj(x, z, mod, w, ln_g, ln_b, layer, j, name):
    return pl.pallas_call(
        _proj_kernel,
        grid=(T_ALL // TM,),
        in_specs=[
            pl.BlockSpec((TM, D), lambda i: (i, 0)),
            pl.BlockSpec((TM, D), lambda i: (i, 0)),
            pl.BlockSpec((None, None, N_MOD, D), lambda i: (_cond_of_tile(i), layer, 0, 0)),
            pl.BlockSpec((None, D, D), lambda i: (j, 0, 0)),
            pl.BlockSpec((None, 1, D), lambda i: (3 * layer + 1, 0, 0)),
            pl.BlockSpec((None, 1, D), lambda i: (3 * layer + 1, 0, 0)),
        ],
        out_specs=pl.BlockSpec((TM, D), lambda i: (i, 0)),
        out_shape=jax.ShapeDtypeStruct((T_ALL, D), f32),
        compiler_params=_params("parallel"),
        name=name,
    )(x, z, mod, w, ln_g, ln_b)


def _rope_tables():
    t = np.arange(DEC_SEQ)
    pos = np.stack([t // GRID_W, t % GRID_W], axis=1).astype(np.float64)
    lane = np.arange(LANES)
    hl = lane % HEAD_DIM
    axis = hl // (HEAD_DIM // 2)
    half = HEAD_DIM // 4
    inv = ROPE_BASE ** (-(hl % half).astype(np.float64) / half)
    ang = pos[:, axis] * inv[None, :]
    first = (hl % (HEAD_DIM // 2)) < half
    cos = np.cos(ang)
    sin = np.where(first[None, :], -np.sin(ang), np.sin(ang))
    return (jnp.asarray(cos, f32), jnp.asarray(sin, f32),
            jnp.asarray(cos[:, :HEAD_DIM].T, f32), jnp.asarray(sin[:, :HEAD_DIM].T, f32))


def _rope_rows(x, cos, sin):
    lane = lax.broadcasted_iota(jnp.int32, (x.shape[0], LANES), 1)
    first = (lane % (HEAD_DIM // 2)) < (HEAD_DIM // 4)
    cols = []
    for c in range(x.shape[1] // LANES):
        xc = x[:, c * LANES:(c + 1) * LANES]
        partner = jnp.where(first, pltpu.roll(xc, LANES - HEAD_DIM // 4, 1), pltpu.roll(xc, HEAD_DIM // 4, 1))
        cols.append(xc * cos + partner * sin)
    return jnp.concatenate(cols, axis=1)


def _rope_cols(xT, cosT, sinT):
    q = HEAD_DIM // 4
    heads = []
    for h in range(xT.shape[0] // HEAD_DIM):
        xh = xT[h * HEAD_DIM:(h + 1) * HEAD_DIM, :]
        partner = jnp.concatenate([xh[q:2 * q], xh[0:q], xh[3 * q:4 * q], xh[2 * q:3 * q]], axis=0)
        heads.append(xh * cosT + partner * sinT)
    return jnp.concatenate(heads, axis=0)


def _qkv_ctx_kernel(x_ref, mod_ref, wqT_ref, wkv_ref, wvT_ref, qT_ref, kv_ref, vT_ref):
    h = _modulate(x_ref, mod_ref, 3)
    qT_ref[...] = (_dot_nt(wqT_ref[...], h) * ATTN_SCALE).astype(bf16)
    kv_ref[...] = jnp.dot(h, wkv_ref[...], preferred_element_type=f32)
    vT_ref[...] = _dot_nt(wvT_ref[...], h).astype(bf16)


def _qkv_ctx(x, mod, wqT, wkv, wvT, layer, j):
    return pl.pallas_call(
        _qkv_ctx_kernel,
        grid=(T_CTX // TM,),
        in_specs=[
            pl.BlockSpec((TM, D), lambda i: (i, 0)),
            pl.BlockSpec((None, None, N_MOD, D), lambda i: (0, layer, 0, 0)),
            pl.BlockSpec((None, Q_DIM, D), lambda i: (j, 0, 0)),
            pl.BlockSpec((None, D, 2 * KV_DIM), lambda i: (j, 0, 0)),
            pl.BlockSpec((None, KV_DIM, D), lambda i: (j, 0, 0)),
        ],
        out_specs=[
            pl.BlockSpec((Q_DIM, TM), lambda i: (0, i)),
            pl.BlockSpec((TM, 2 * KV_DIM), lambda i: (i, 0)),
            pl.BlockSpec((KV_DIM, TM), lambda i: (0, i)),
        ],
        out_shape=[
            jax.ShapeDtypeStruct((Q_DIM, T_CTX), bf16),
            jax.ShapeDtypeStruct((T_CTX, 2 * KV_DIM), f32),
            jax.ShapeDtypeStruct((KV_DIM, T_CTX), bf16),
        ],
        compiler_params=_params("parallel"),
        name=f"qkv_ctx_l{layer}",
    )(x, mod, wqT, wkv, wvT)


def _qkv_lat_kernel(x_ref, mod_ref, wqT_ref, wkv_ref, wvT_ref, cos_ref, sin_ref, cosT_ref, sinT_ref,
                    qT_ref, k_ref, vT_ref):
    h = _modulate(x_ref, mod_ref, 3)
    qT = _rope_cols(_dot_nt(wqT_ref[...], h), cosT_ref[...], sinT_ref[...])
    qT_ref[...] = (qT * ATTN_SCALE).astype(bf16)
    k = jnp.dot(h, wkv_ref[:, :KV_DIM], preferred_element_type=f32)
    k_ref[...] = _rope_rows(k, cos_ref[...], sin_ref[...]).astype(bf16)
    vT_ref[...] = _dot_nt(wvT_ref[...], h).astype(bf16)


def _qkv_lat(x, mod, wqT, wkv, wvT, rope, layer, j):
    row0 = T_CTX // DEC_SEQ
    full = lambda shape: pl.BlockSpec(shape, lambda b: (0, 0))
    return pl.pallas_call(
        _qkv_lat_kernel,
        grid=(DEC_BATCH,),
        in_specs=[
            pl.BlockSpec((DEC_SEQ, D), lambda b: (row0 + b, 0)),
            pl.BlockSpec((None, None, N_MOD, D), lambda b: (1 + b, layer, 0, 0)),
            pl.BlockSpec((None, Q_DIM, D), lambda b: (j, 0, 0)),
            pl.BlockSpec((None, D, 2 * KV_DIM), lambda b: (j, 0, 0)),
            pl.BlockSpec((None, KV_DIM, D), lambda b: (j, 0, 0)),
            full((DEC_SEQ, LANES)), full((DEC_SEQ, LANES)), full((HEAD_DIM, DEC_SEQ)), full((HEAD_DIM, DEC_SEQ)),
        ],
        out_specs=[
            pl.BlockSpec((Q_DIM, DEC_SEQ), lambda b: (0, b)),
            pl.BlockSpec((DEC_SEQ, KV_DIM), lambda b: (b, 0)),
            pl.BlockSpec((KV_DIM, DEC_SEQ), lambda b: (0, b)),
        ],
        out_shape=[
            jax.ShapeDtypeStruct((Q_DIM, T_LAT), bf16),
            jax.ShapeDtypeStruct((T_LAT, KV_DIM), bf16),
            jax.ShapeDtypeStruct((KV_DIM, T_LAT), bf16),
        ],
        compiler_params=_params("parallel"),
        name=f"qkv_lat_l{layer}",
    )(x, mod, wqT, wkv, wvT, *rope)


def _softmax_pv_t(parts, sink_row):
    m = sink_row
    for s, _ in parts:
        m = jnp.maximum(m, jnp.max(s, axis=0, keepdims=True))
    denom = jnp.exp(sink_row - m)
    o = None
    for s, vT in parts:
        p = jnp.exp(s - m)
        denom = denom + jnp.sum(p, axis=0, keepdims=True)
        pv = jnp.dot(vT, p.astype(bf16), preferred_element_type=f32)
        o = pv if o is None else o + pv
    return o / denom


def _group_queries(qT_ref, g):
    heads = [qT_ref[(Q_PER_KV * g + i) * HEAD_DIM:(Q_PER_KV * g + i + 1) * HEAD_DIM, :] for i in range(Q_PER_KV)]
    return jnp.concatenate(heads, axis=1)


def _sink_row(sink_ref, g, nq):
    return jnp.concatenate([jnp.full((1, nq), sink_ref[Q_PER_KV * g + i], f32) for i in range(Q_PER_KV)], axis=1)


def _store_heads(o_ref, group_outs, nq):
    heads = [oT[:, i * nq:(i + 1) * nq] for oT in group_outs for i in range(Q_PER_KV)]
    o_ref[...] = jnp.concatenate(heads, axis=0).T.astype(o_ref.dtype)


def _attn_ctx_kernel(sink_ref, qT_ref, kv_ref, vT_ref, o_ref):
    k = kv_ref[:, :KV_DIM].astype(bf16)
    outs = []
    for g in range(KV_HEADS):
        gs = slice(g * HEAD_DIM, (g + 1) * HEAD_DIM)
        s = jnp.dot(k[:, gs], _group_queries(qT_ref, g), preferred_element_type=f32)
        outs.append(_softmax_pv_t([(s, vT_ref[gs, :])], _sink_row(sink_ref, g, SEQ)))
    _store_heads(o_ref, outs, SEQ)


def _attn_ctx(qT, kv, vT, sink):
    return pl.pallas_call(
        _attn_ctx_kernel,
        grid=(BATCH,),
        in_specs=[
            pl.BlockSpec(memory_space=pltpu.SMEM),
            pl.BlockSpec((Q_DIM, SEQ), lambda b: (0, b)),
            pl.BlockSpec((SEQ, 2 * KV_DIM), lambda b: (b, 0)),
            pl.BlockSpec((KV_DIM, SEQ), lambda b: (0, b)),
        ],
        out_specs=pl.BlockSpec((SEQ, D), lambda b: (b, 0)),
        out_shape=jax.ShapeDtypeStruct((T_ALL, D), bf16),
        compiler_params=_params("parallel"),
        name="attn_ctx",
    )(sink, qT, kv, vT)


def _attn_lat_kernel(sink_ref, qT_ref, kp_ref, kc_ref, kn_ref, vp_ref, vc_ref, vn_ref, ck_ref, cv_ref, o_in_ref,
                     o_ref, cvT_scr):
    del o_in_ref
    qb = pl.program_id(1)

    @pl.when(qb == 0)
    def _():
        cvT_scr[...] = cv_ref[...].T.astype(bf16)

    n = Q_PER_KV * BLOCK
    row = lax.broadcasted_iota(jnp.int32, (WIN_KEYS, n), 0)
    ql = lax.broadcasted_iota(jnp.int32, (WIN_KEYS, n), 1) & (BLOCK - 1)
    kj = (qb - 1) * BLOCK + row
    rel = row - ql
    valid = (rel >= 0) & (rel <= 2 * WINDOW) & (kj >= 0) & (kj < DEC_SEQ)
    k_win = jnp.concatenate([kp_ref[...], kc_ref[...], kn_ref[...]], axis=0)
    k_ctx = ck_ref[...].astype(bf16)
    outs = []
    for g in range(KV_HEADS):
        gs = slice(g * HEAD_DIM, (g + 1) * HEAD_DIM)
        qg = _group_queries(qT_ref, g)
        s_loc = jnp.where(valid, jnp.dot(k_win[:, gs], qg, preferred_element_type=f32), NEG_INF)
        s_ctx = jnp.dot(k_ctx[:, gs], qg, preferred_element_type=f32)
        v_win = jnp.concatenate([vp_ref[gs, :], vc_ref[gs, :], vn_ref[gs, :]], axis=1)
        outs.append(_softmax_pv_t([(s_loc, v_win), (s_ctx, cvT_scr[gs, :])], _sink_row(sink_ref, g, BLOCK)))
    _store_heads(o_ref, outs, BLOCK)


def _attn_lat(qT, k, vT, o_ctx, cache_k, cache_v, sink, j):
    nqb = DEC_SEQ // BLOCK
    prev = lambda b, i: b * nqb + jnp.maximum(i - 1, 0)
    own = lambda b, i: b * nqb + i
    nxt = lambda b, i: b * nqb + jnp.minimum(i + 1, nqb - 1)
    return pl.pallas_call(
        _attn_lat_kernel,
        grid=(DEC_BATCH, nqb),
        in_specs=[
            pl.BlockSpec(memory_space=pltpu.SMEM),
            pl.BlockSpec((Q_DIM, BLOCK), lambda b, i: (0, own(b, i))),
            pl.BlockSpec((BLOCK, KV_DIM), lambda b, i: (prev(b, i), 0)),
            pl.BlockSpec((BLOCK, KV_DIM), lambda b, i: (own(b, i), 0)),
            pl.BlockSpec((BLOCK, KV_DIM), lambda b, i: (nxt(b, i), 0)),
            pl.BlockSpec((KV_DIM, BLOCK), lambda b, i: (0, prev(b, i))),
            pl.BlockSpec((KV_DIM, BLOCK), lambda b, i: (0, own(b, i))),
            pl.BlockSpec((KV_DIM, BLOCK), lambda b, i: (0, nxt(b, i))),
            pl.BlockSpec((None, None, SEQ, KV_DIM), lambda b, i: (b, j, 0, 0)),
            pl.BlockSpec((None, None, SEQ, KV_DIM), lambda b, i: (b, j, 0, 0)),
            pl.BlockSpec(memory_space=pl.ANY),
        ],
        out_specs=pl.BlockSpec((BLOCK, D), lambda b, i: (T_CTX // BLOCK + own(b, i), 0)),
        out_shape=jax.ShapeDtypeStruct((T_ALL, D), bf16),
        scratch_shapes=[pltpu.VMEM((KV_DIM, SEQ), bf16)],
        input_output_aliases={10: 0},
        compiler_params=_params("parallel", "arbitrary"),
        name="attn_lat",
    )(sink, qT, k, k, k, vT, vT, vT, cache_k, cache_v, o_ctx)


def _dft_tables(L):
    idx = np.arange(L)
    ft = np.outer(idx, idx) % (2 * L)
    cm = np.cos(np.pi * ft / L)
    sm = np.sin(np.pi * ft / L)
    nyq = np.where(idx % 2 == 0, 1.0, -1.0)
    sm_n = sm.copy()
    sm_n[0, :] = nyq
    fwd = np.concatenate([cm, sm_n], axis=0)
    wc = np.full((L,), 2.0)
    wc[0] = 1.0
    inv_c = cm.T * wc[None, :]
    inv_s = sm.T * 2.0
    inv_s[:, 0] = nyq
    inv = np.concatenate([inv_c, inv_s], axis=1) / (2 * L)
    return cm, sm, nyq, fwd, inv


def _filter_tables(L):
    t = np.arange(L, dtype=np.float64) / L
    bands = np.arange(1, FILTER_BANDS + 1, dtype=np.float64)
    ph = 2 * np.pi * t[:, None] * bands[None]
    feats = np.zeros((L, LANES))
    feats[:, :FILTER_EMB] = np.concatenate([t[:, None], np.sin(ph), np.cos(ph)], -1)
    max_decay = math.log(DECAY_TARGET) / DECAY_FAST_PCT
    min_decay = math.log(DECAY_TARGET) / DECAY_SLOW_PCT
    deltas = np.abs(np.linspace(min_decay, max_decay, D))
    decay = np.exp(-t[:, None] * deltas[None])
    return feats, decay


def _dot_hi(a, b):
    return jnp.dot(a, b, preferred_element_type=f32, precision=lax.Precision.HIGHEST)


def _filter_kernel(feat_ref, w1_ref, b1_ref, w2_ref, b2_ref, w3f_ref, w3b_ref, dec_ref, cm_ref, sm_ref, nyq_ref,
                   o_ref):
    a = jnp.sin(_dot_hi(feat_ref[...], w1_ref[...]) + b1_ref[...])
    a = jnp.sin(_dot_hi(a, w2_ref[...]) + b2_ref[...])
    dec = dec_ref[...]
    fwd = _dot_hi(a, w3f_ref[...]) * dec
    bwd = _dot_hi(a, w3b_ref[...]) * dec
    row = lax.broadcasted_iota(jnp.int32, fwd.shape, 0)
    bwd = jnp.where(row == 0, 0.0, bwd)
    norm = jnp.sum(jnp.abs(fwd), axis=0, keepdims=True) + jnp.sum(jnp.abs(bwd), axis=0, keepdims=True) + 1e-6
    even = (fwd + bwd) / norm
    odd = (bwd - fwd) / norm
    hr = _dot_hi(cm_ref[...], even)
    hi = _dot_hi(sm_ref[...], odd)
    h_nyq = jnp.sum(even * nyq_ref[...], axis=0, keepdims=True)
    o_ref[0] = hr
    o_ref[1] = jnp.where(row == 0, h_nyq, hi)


def _hyena_filter(L, j, w1p, b1p, w2p, b2p, w3p, name):
    cm, sm, nyq, _, _ = _dft_tables(L)
    feats, decay = _filter_tables(L)
    td = 512
    nd = D // td
    full = lambda shape: pl.BlockSpec(shape, lambda o, d: (0,) * len(shape))
    return pl.pallas_call(
        _filter_kernel,
        grid=(2, nd),
        in_specs=[
            full((L, LANES)),
            pl.BlockSpec((None, LANES, LANES), lambda o, d: (j, 0, 0)),
            pl.BlockSpec((None, 1, LANES), lambda o, d: (j, 0, 0)),
            pl.BlockSpec((None, LANES, LANES), lambda o, d: (j, 0, 0)),
            pl.BlockSpec((None, 1, LANES), lambda o, d: (j, 0, 0)),
            pl.BlockSpec((None, LANES, td), lambda o, d: (j, 0, (2 * o) * nd + d)),
            pl.BlockSpec((None, LANES, td), lambda o, d: (j, 0, (2 * o + 1) * nd + d)),
            pl.BlockSpec((L, td), lambda o, d: (0, d)),
            full((L, L)),
            full((L, L)),
            full((L, 1)),
        ],
        out_specs=pl.BlockSpec((2, L, td), lambda o, d: (o, 0, d)),
        out_shape=jax.ShapeDtypeStruct((4, L, D), f32),
        compiler_params=_params("parallel", "parallel"),
        name=name,
    )(jnp.asarray(feats, f32), w1p, b1p, w2p, b2p, w3p, w3p, jnp.asarray(decay, f32),
      jnp.asarray(cm, f32), jnp.asarray(sm, f32), jnp.asarray(nyq[:, None], f32))


def _short_conv(u, w, b):
    L = u.shape[0]
    row = lax.broadcasted_iota(jnp.int32, u.shape, 0)
    prev = jnp.where(row == 0, 0.0, pltpu.roll(u, 1, 0))
    nxt = jnp.where(row == L - 1, 0.0, pltpu.roll(u, L - 1, 0))
    return prev * w[0:1, :] + u * w[1:2, :] + nxt * w[2:3, :] + b


def _long_conv(z, hr, hi, fwd, inv):
    L = z.shape[0]
    zs = jnp.dot(fwd, z.astype(bf16), preferred_element_type=f32)
    zr, zi = zs[:L], zs[L:]
    row = lax.broadcasted_iota(jnp.int32, hr.shape, 0)
    hi_m = jnp.where(row == 0, 0.0, hi)
    yr = zr * hr + zi * hi_m
    yi = zi * jnp.where(row == 0, hi, hr) - zr * hi_m
    ys = jnp.concatenate([yr, yi], axis=0).astype(bf16)
    return jnp.dot(inv, ys, preferred_element_type=f32)


def _hyena_conv_kernel(pv_ref, p1_ref, p2_ref, wv_ref, w1_ref, w2_ref, bv_ref, b1_ref, b2_ref, h_ref, d_ref,
                       fwd_ref, inv_ref, o_in_ref, o_ref):
    del o_in_ref
    v = _short_conv(pv_ref[...], wv_ref[...], bv_ref[...])
    x1 = _short_conv(p1_ref[...], w1_ref[...], b1_ref[...])
    x2 = _short_conv(p2_ref[...], w2_ref[...], b2_ref[...])
    fwd, inv = fwd_ref[...], inv_ref[...]
    z = x1 * (_long_conv(v, h_ref[0], h_ref[1], fwd, inv) + v * d_ref[0:1, :])
    z = x2 * (_long_conv(z, h_ref[2], h_ref[3], fwd, inv) + z * d_ref[1:2, :])
    o_ref[...] = z.astype(bf16)


def _hyena_conv(p, z_in, conv_w, conv_b, spec, hy_d, j, L, nb, row0, td, name):
    _, _, _, fwd, inv = _dft_tables(L)
    nd = D // td
    in_specs = []
    for c in range(3):
        in_specs.append(pl.BlockSpec((L, td), lambda d, b, c=c: (row0 + b, c * nd + d)))
    for c in range(3):
        in_specs.append(pl.BlockSpec((None, 3, td), lambda d, b, c=c: (j, 0, c * nd + d)))
    for c in range(3):
        in_specs.append(pl.BlockSpec((None, 1, td), lambda d, b, c=c: (j, 0, c * nd + d)))
    in_specs += [
        pl.BlockSpec((4, L, td), lambda d, b: (0, 0, d)),
        pl.BlockSpec((None, 2, td), lambda d, b: (j, 0, d)),
        pl.BlockSpec((2 * L, L), lambda d, b: (0, 0)),
        pl.BlockSpec((L, 2 * L), lambda d, b: (0, 0)),
    ]
    args = [p, p, p, conv_w, conv_w, conv_w, conv_b, conv_b, conv_b, spec, hy_d,
            jnp.asarray(fwd, f32).astype(bf16), jnp.asarray(inv, f32).astype(bf16)]
    aliases = {}
    if z_in is not None:
        in_specs.append(pl.BlockSpec(memory_space=pl.ANY))
        args.append(z_in)
        aliases = {len(args) - 1: 0}
        kernel = _hyena_conv_kernel
    else:
        kernel = lambda *refs: _hyena_conv_kernel(*refs[:-1], None, refs[-1])
    return pl.pallas_call(
        kernel,
        grid=(nd, nb),
        in_specs=in_specs,
        out_specs=pl.BlockSpec((L, td), lambda d, b: (row0 + b, d)),
        out_shape=jax.ShapeDtypeStruct((T_ALL, D), bf16),
        input_output_aliases=aliases,
        compiler_params=_params("parallel", "parallel"),
        name=name,
    )(*args)


def _pad2(a, rows, cols):
    return jnp.pad(a, [(0, 0)] * (a.ndim - 2) + [(0, rows - a.shape[-2]), (0, cols - a.shape[-1])])


def kernel(x_prompt, x_sample, cache_k, cache_v, c, c_ctx, ada_w, ada_b, ln_g, ln_b, ffn_w1, ffn_w2, attn_w_qkv,
           attn_w_o, attn_sink, hy_w_in, hy_conv_w, hy_conv_b, hy_f_w1, hy_f_b1, hy_f_w2, hy_f_b2, hy_f_w3, hy_d,
           hy_w_out):
    x = jnp.concatenate([x_prompt.reshape(T_CTX, D), x_sample.reshape(T_LAT, D)], axis=0)
    cond = jnp.concatenate([c_ctx[None], c, jnp.zeros((N_COND - 1 - DEC_BATCH, D), f32)], axis=0)
    mod = _ada(cond, ada_w, ada_b)

    w1 = ffn_w1.astype(bf16)
    w2 = ffn_w2.astype(bf16)
    w_qkv = attn_w_qkv.astype(bf16)
    wqT = jnp.swapaxes(w_qkv[:, :, :Q_DIM], 1, 2)
    wkv = w_qkv[:, :, Q_DIM:]
    wvT = jnp.swapaxes(w_qkv[:, :, Q_DIM + KV_DIM:], 1, 2)
    w_o = attn_w_o.astype(bf16)
    w_in = hy_w_in.astype(bf16)
    w_out = hy_w_out.astype(bf16)
    ck = cache_k.reshape(DEC_BATCH, -1, SEQ, KV_DIM)
    cv = cache_v.reshape(DEC_BATCH, -1, SEQ, KV_DIM)
    rope = _rope_tables()
    fw1 = _pad2(hy_f_w1, LANES, LANES)
    fb1 = _pad2(hy_f_b1[:, None, :], 1, LANES)
    fw2 = _pad2(hy_f_w2, LANES, LANES)
    fb2 = _pad2(hy_f_b2[:, None, :], 1, LANES)
    fw3 = _pad2(hy_f_w3, LANES, hy_f_w3.shape[-1])
    conv_b = hy_conv_b[:, None, :]
    ln_g = ln_g.reshape(DEPTH * 3, 1, D)
    ln_b = ln_b.reshape(DEPTH * 3, 1, D)

    new_k, new_v = [], []
    for i in range(DEPTH):
        j = i // 2
        x = _ffn(x, mod, w1, w2, ln_g, ln_b, i, 0)
        if i % 2 == 0:
            qT_c, kv_c, vT_c = _qkv_ctx(x, mod, wqT, wkv, wvT, i, j)
            qT_l, k_l, vT_l = _qkv_lat(x, mod, wqT, wkv, wvT, rope, i, j)
            new_k.append(kv_c[:, :KV_DIM].reshape(BATCH, SEQ, KV_HEADS, HEAD_DIM))
            new_v.append(kv_c[:, KV_DIM:].reshape(BATCH, SEQ, KV_HEADS, HEAD_DIM))
            sink = attn_sink[j]
            z = _attn_ctx(qT_c, kv_c, vT_c, sink)
            z = _attn_lat(qT_l, k_l, vT_l, z, ck, cv, sink, j)
            x = _proj(x, z, mod, w_o, ln_g, ln_b, i, j, f"attn_out_l{i}")
        else:
            p = _modmm(x, mod, w_in, i, j, f"hy_in_l{i}")
            spec_ctx = _hyena_filter(SEQ, j, fw1, fb1, fw2, fb2, fw3, f"hy_filter_ctx_l{i}")
            spec_lat = _hyena_filter(DEC_SEQ, j, fw1, fb1, fw2, fb2, fw3, f"hy_filter_lat_l{i}")
            z = _hyena_conv(p, None, hy_conv_w, conv_b, spec_ctx, hy_d, j, SEQ, BATCH, 0, 512, f"hy_conv_ctx_l{i}")
            z = _hyena_conv(p, z, hy_conv_w, conv_b, spec_lat, hy_d, j, DEC_SEQ, DEC_BATCH, T_CTX // DEC_SEQ, 256,
                            f"hy_conv_lat_l{i}")
            x = _proj(x, z, mod, w_out, ln_g, ln_b, i, j, f"hy_out_l{i}")
        x = _ffn(x, mod, w1, w2, ln_g, ln_b, i, 1)

    y_prompt = x[:T_CTX].reshape(BATCH, SEQ, D)
    y_sample = x[T_CTX:].reshape(DEC_BATCH, DEC_SEQ, D)
    return y_prompt, y_sample, jnp.stack(new_k, axis=1), jnp.stack(new_v, axis=1)
```

```python
import functools
import math

import numpy as np
import jax
import jax.numpy as jnp
from jax import lax
from jax.experimental import pallas as pl
from jax.experimental.pallas import tpu as pltpu

D = 1024
BATCH, SEQ = 16, 256
DEC_BATCH, DEC_SEQ = 8, 1024
DEPTH = 4
N_HEADS, HEAD_DIM, KV_HEADS = 16, 64, 4
Q_PER_KV = N_HEADS // KV_HEADS
Q_DIM = N_HEADS * HEAD_DIM
KV_DIM = KV_HEADS * HEAD_DIM
GRID_W = 64
WINDOW = 128
BLOCK = 128
WIN_KEYS = BLOCK + 2 * WINDOW
ROPE_BASE = 10000.0
ATTN_SCALE = HEAD_DIM ** -0.5
D_FF = 2816
N_MOD = 9
LN_EPS = 1e-5
ALPHA = (2 * DEPTH) ** 0.25
NEG_INF = -1e30
FILTER_BANDS = 16
FILTER_EMB = 1 + 2 * FILTER_BANDS
DECAY_FAST_PCT, DECAY_SLOW_PCT, DECAY_TARGET = 0.3, 1.5, 1e-2

T_CTX = BATCH * SEQ
T_LAT = DEC_BATCH * DEC_SEQ
T_ALL = T_CTX + T_LAT
N_COND = 16
TM = 1024
FFN_TM = 512
FFN_SUB = 256
LANES = 128
VMEM_LIMIT = 56 * 1024 * 1024

f32 = jnp.float32
bf16 = jnp.bfloat16


def _cond_of_tile(i, tm=TM):
    return jnp.where(i < T_CTX // tm, 0, 1 + (i * tm - T_CTX) // DEC_SEQ)


def _params(*sem):
    return pltpu.CompilerParams(dimension_semantics=sem, vmem_limit_bytes=VMEM_LIMIT)


def _layer_norm(y, g, b):
    mu = jnp.mean(y, axis=-1, keepdims=True)
    yc = y - mu
    var = jnp.mean(yc * yc, axis=-1, keepdims=True)
    return yc * lax.rsqrt(var + LN_EPS) * g + b


def _modulate(x_ref, mod_ref, base):
    shift = mod_ref[base:base + 1, :]
    scale = mod_ref[base + 1:base + 2, :]
    return (x_ref[...] * (1.0 + scale) + shift).astype(bf16)


def _dot_nt(a, b):
    return lax.dot_general(a, b, (((1,), (1,)), ((), ())), preferred_element_type=f32)


def _ada_kernel(c_ref, w_ref, b_ref, o_ref):
    s = jax.nn.silu(c_ref[...]).astype(bf16)
    o_ref[...] = jnp.dot(s, w_ref[...].astype(bf16), preferred_element_type=f32) + b_ref[...]


def _ada(cond, ada_w, ada_b):
    tn = 2304
    nt = (N_MOD * D) // tn
    out = pl.pallas_call(
        _ada_kernel,
        grid=(DEPTH, nt),
        in_specs=[
            pl.BlockSpec((N_COND, D), lambda l, n: (0, 0)),
            pl.BlockSpec((None, D, tn), lambda l, n: (l, 0, n)),
            pl.BlockSpec((None, 1, tn), lambda l, n: (l, 0, n)),
        ],
        out_specs=pl.BlockSpec((N_COND, tn), lambda l, n: (0, l * nt + n)),
        out_shape=jax.ShapeDtypeStruct((N_COND, DEPTH * N_MOD * D), f32),
        compiler_params=_params("arbitrary", "arbitrary"),
        name="ada_mod",
    )(cond, ada_w, ada_b.reshape(DEPTH, 1, N_MOD * D))
    return out.reshape(N_COND, DEPTH, N_MOD, D)


def _ffn_kernel(x_ref, mod_ref, w1_ref, w2_ref, g_ref, b_ref, o_ref, *, mod_base):
    shift = mod_ref[mod_base:mod_base + 1, :]
    scale = mod_ref[mod_base + 1:mod_base + 2, :]
    half_gate = 0.5 * mod_ref[mod_base + 2:mod_base + 3, :]
    for r in range(FFN_TM // FFN_SUB):
        rows = slice(r * FFN_SUB, (r + 1) * FFN_SUB)
        x = x_ref[rows, :]
        h = (x * (1.0 + scale) + shift).astype(bf16)
        gu = jnp.dot(h, w1_ref[...], preferred_element_type=f32)
        a = (jax.nn.silu(gu[:, :D_FF]) * gu[:, D_FF:]).astype(bf16)
        f = jnp.dot(a, w2_ref[...], preferred_element_type=f32)
        o_ref[rows, :] = _layer_norm(ALPHA * x + half_gate * f, g_ref[...], b_ref[...])


def _ffn(x, mod, w1, w2, ln_g, ln_b, layer, half):
    resident = pl.Buffered(1)
    return pl.pallas_call(
        functools.partial(_ffn_kernel, mod_base=6 * half),
        grid=(T_ALL // FFN_TM,),
        in_specs=[
            pl.BlockSpec((FFN_TM, D), lambda i: (i, 0)),
            pl.BlockSpec((None, None, N_MOD, D), lambda i: (_cond_of_tile(i, FFN_TM), layer, 0, 0)),
            pl.BlockSpec((None, None, D, 2 * D_FF), lambda i: (layer, half, 0, 0), pipeline_mode=resident),
            pl.BlockSpec((None, None, D_FF, D), lambda i: (layer, half, 0, 0), pipeline_mode=resident),
            pl.BlockSpec((None, 1, D), lambda i: (3 * layer + 2 * half, 0, 0)),
            pl.BlockSpec((None, 1, D), lambda i: (3 * layer + 2 * half, 0, 0)),
        ],
        out_specs=pl.BlockSpec((FFN_TM, D), lambda i: (i, 0)),
        out_shape=jax.ShapeDtypeStruct((T_ALL, D), f32),
        compiler_params=_params("parallel"),
        name=f"ffn_l{layer}h{half}",
    )(x, mod, w1, w2, ln_g, ln_b)


def _modmm_kernel(x_ref, mod_ref, w_ref, o_ref):
    o_ref[...] = jnp.dot(_modulate(x_ref, mod_ref, 3), w_ref[...], preferred_element_type=f32)


def _modmm(x, mod, w, layer, j, name):
    n = w.shape[-1]
    tn = 1536
    return pl.pallas_call(
        _modmm_kernel,
        grid=(T_ALL // TM, n // tn),
        in_specs=[
            pl.BlockSpec((TM, D), lambda i, c: (i, 0)),
            pl.BlockSpec((None, None, N_MOD, D), lambda i, c: (_cond_of_tile(i), layer, 0, 0)),
            pl.BlockSpec((None, D, tn), lambda i, c: (j, 0, c)),
        ],
        out_specs=pl.BlockSpec((TM, tn), lambda i, c: (i, c)),
        out_shape=jax.ShapeDtypeStruct((T_ALL, n), f32),
        compiler_params=_params("parallel", "arbitrary"),
        name=name,
    )(x, mod, w)


def _proj_kernel(x_ref, z_ref, mod_ref, w_ref, g_ref, b_ref, o_ref):
    gate = mod_ref[5:6, :]
    f = jnp.dot(z_ref[...], w_ref[...], preferred_element_type=f32)
    y = ALPHA * x_ref[...] + gate * f
    o_ref[...] = _layer_norm(y, g_ref[...], b_ref[...])


def _proj(x, z, mod, w, ln_g, ln_b, layer, j, name):
    return pl.pallas_call(
        _proj_kernel,
        grid=(T_ALL // TM,),
        in_specs=[
            pl.BlockSpec((TM, D), lambda i: (i, 0)),
            pl.BlockSpec((TM, D), lambda i: (i, 0)),
            pl.BlockSpec((None, None, N_MOD, D), lambda i: (_cond_of_tile(i), layer, 0, 0)),
            pl.BlockSpec((None, D, D), lambda i: (j, 0, 0)),
            pl.BlockSpec((None, 1, D), lambda i: (3 * layer + 1, 0, 0)),
            pl.BlockSpec((None, 1, D), lambda i: (3 * layer + 1, 0, 0)),
        ],
        out_specs=pl.BlockSpec((TM, D), lambda i: (i, 0)),
        out_shape=jax.ShapeDtypeStruct((T_ALL, D), f32),
        compiler_params=_params("parallel"),
        name=name,
    )(x, z, mod, w, ln_g, ln_b)


def _rope_tables():
    t = np.arange(DEC_SEQ)
    pos = np.stack([t // GRID_W, t % GRID_W], axis=1).astype(np.float64)
    lane = np.arange(LANES)
    hl = lane % HEAD_DIM
    axis = hl // (HEAD_DIM // 2)
    half = HEAD_DIM // 4
    inv = ROPE_BASE ** (-(hl % half).astype(np.float64) / half)
    ang = pos[:, axis] * inv[None, :]
    first = (hl % (HEAD_DIM // 2)) < half
    cos = np.cos(ang)
    sin = np.where(first[None, :], -np.sin(ang), np.sin(ang))
    return (jnp.asarray(cos, f32), jnp.asarray(sin, f32),
            jnp.asarray(cos[:, :HEAD_DIM].T, f32), jnp.asarray(sin[:, :HEAD_DIM].T, f32))


def _rope_rows(x, cos, sin):
    lane = lax.broadcasted_iota(jnp.int32, (x.shape[0], LANES), 1)
    first = (lane % (HEAD_DIM // 2)) < (HEAD_DIM // 4)
    cols = []
    for c in range(x.shape[1] // LANES):
        xc = x[:, c * LANES:(c + 1) * LANES]
        partner = jnp.where(first, pltpu.roll(xc, LANES - HEAD_DIM // 4, 1), pltpu.roll(xc, HEAD_DIM // 4, 1))
        cols.append(xc * cos + partner * sin)
    return jnp.concatenate(cols, axis=1)


def _rope_cols(xT, cosT, sinT):
    q = HEAD_DIM // 4
    heads = []
    for h in range(xT.shape[0] // HEAD_DIM):
        xh = xT[h * HEAD_DIM:(h + 1) * HEAD_DIM, :]
        partner = jnp.concatenate([xh[q:2 * q], xh[0:q], xh[3 * q:4 * q], xh[2 * q:3 * q]], axis=0)
        heads.append(xh * cosT + partner * sinT)
    return jnp.concatenate(heads, axis=0)


def _qkv_ctx_kernel(x_ref, mod_ref, wqT_ref, wkv_ref, wvT_ref, qT_ref, kv_ref, vT_ref):
    h = _modulate(x_ref, mod_ref, 3)
    qT_ref[...] = (_dot_nt(wqT_ref[...], h) * ATTN_SCALE).astype(bf16)
    kv_ref[...] = jnp.dot(h, wkv_ref[...], preferred_element_type=f32)
    vT_ref[...] = _dot_nt(wvT_ref[...], h).astype(bf16)


def _qkv_ctx(x, mod, wqT, wkv, wvT, layer, j):
    return pl.pallas_call(
        _qkv_ctx_kernel,
        grid=(T_CTX // TM,),
        in_specs=[
            pl.BlockSpec((TM, D), lambda i: (i, 0)),
            pl.BlockSpec((None, None, N_MOD, D), lambda i: (0, layer, 0, 0)),
            pl.BlockSpec((None, Q_DIM, D), lambda i: (j, 0, 0)),
            pl.BlockSpec((None, D, 2 * KV_DIM), lambda i: (j, 0, 0)),
            pl.BlockSpec((None, KV_DIM, D), lambda i: (j, 0, 0)),
        ],
        out_specs=[
            pl.BlockSpec((Q_DIM, TM), lambda i: (0, i)),
            pl.BlockSpec((TM, 2 * KV_DIM), lambda i: (i, 0)),
            pl.BlockSpec((KV_DIM, TM), lambda i: (0, i)),
        ],
        out_shape=[
            jax.ShapeDtypeStruct((Q_DIM, T_CTX), bf16),
            jax.ShapeDtypeStruct((T_CTX, 2 * KV_DIM), f32),
            jax.ShapeDtypeStruct((KV_DIM, T_CTX), bf16),
        ],
        compiler_params=_params("parallel"),
        name=f"qkv_ctx_l{layer}",
    )(x, mod, wqT, wkv, wvT)


def _qkv_lat_kernel(x_ref, mod_ref, wqT_ref, wkv_ref, wvT_ref, cos_ref, sin_ref, cosT_ref, sinT_ref,
                    qT_ref, k_ref, vT_ref):
    h = _modulate(x_ref, mod_ref, 3)
    qT = _rope_cols(_dot_nt(wqT_ref[...], h), cosT_ref[...], sinT_ref[...])
    qT_ref[...] = (qT * ATTN_SCALE).astype(bf16)
    k = jnp.dot(h, wkv_ref[:, :KV_DIM], preferred_element_type=f32)
    k_ref[...] = _rope_rows(k, cos_ref[...], sin_ref[...]).astype(bf16)
    vT_ref[...] = _dot_nt(wvT_ref[...], h).astype(bf16)


def _qkv_lat(x, mod, wqT, wkv, wvT, rope, layer, j):
    row0 = T_CTX // DEC_SEQ
    full = lambda shape: pl.BlockSpec(shape, lambda b: (0, 0))
    return pl.pallas_call(
        _qkv_lat_kernel,
        grid=(DEC_BATCH,),
        in_specs=[
            pl.BlockSpec((DEC_SEQ, D), lambda b: (row0 + b, 0)),
            pl.BlockSpec((None, None, N_MOD, D), lambda b: (1 + b, layer, 0, 0)),
            pl.BlockSpec((None, Q_DIM, D), lambda b: (j, 0, 0)),
            pl.BlockSpec((None, D, 2 * KV_DIM), lambda b: (j, 0, 0)),
            pl.BlockSpec((None, KV_DIM, D), lambda b: (j, 0, 0)),
            full((DEC_SEQ, LANES)), full((DEC_SEQ, LANES)), full((HEAD_DIM, DEC_SEQ)), full((HEAD_DIM, DEC_SEQ)),
        ],
        out_specs=[
            pl.BlockSpec((Q_DIM, DEC_SEQ), lambda b: (0, b)),
            pl.BlockSpec((DEC_SEQ, KV_DIM), lambda b: (b, 0)),
            pl.BlockSpec((KV_DIM, DEC_SEQ), lambda b: (0, b)),
        ],
        out_shape=[
            jax.ShapeDtypeStruct((Q_DIM, T_LAT), bf16),
            jax.ShapeDtypeStruct((T_LAT, KV_DIM), bf16),
            jax.ShapeDtypeStruct((KV_DIM, T_LAT), bf16),
        ],
        compiler_params=_params("parallel"),
        name=f"qkv_lat_l{layer}",
    )(x, mod, wqT, wkv, wvT, *rope)


def _softmax_pv_t(parts, sink_row):
    m = sink_row
    for s, _ in parts:
        m = jnp.maximum(m, jnp.max(s, axis=0, keepdims=True))
    denom = jnp.exp(sink_row - m)
    o = None
    for s, vT in parts:
        p = jnp.exp(s - m)
        denom = denom + jnp.sum(p, axis=0, keepdims=True)
        pv = jnp.dot(vT, p.astype(bf16), preferred_element_type=f32)
        o = pv if o is None else o + pv
    return o / denom


def _group_queries(qT_ref, g):
    heads = [qT_ref[(Q_PER_KV * g + i) * HEAD_DIM:(Q_PER_KV * g + i + 1) * HEAD_DIM, :] for i in range(Q_PER_KV)]
    return jnp.concatenate(heads, axis=1)


def _sink_row(sink_ref, g, nq):
    return jnp.concatenate([jnp.full((1, nq), sink_ref[Q_PER_KV * g + i], f32) for i in range(Q_PER_KV)], axis=1)


def _store_heads(o_ref, group_outs, nq):
    heads = [oT[:, i * nq:(i + 1) * nq] for oT in group_outs for i in range(Q_PER_KV)]
    o_ref[...] = jnp.concatenate(heads, axis=0).T.astype(o_ref.dtype)


def _attn_ctx_kernel(sink_ref, qT_ref, kv_ref, vT_ref, o_ref):
    k = kv_ref[:, :KV_DIM].astype(bf16)
    outs = []
    for g in range(KV_HEADS):
        gs = slice(g * HEAD_DIM, (g + 1) * HEAD_DIM)
        s = jnp.dot(k[:, gs], _group_queries(qT_ref, g), preferred_element_type=f32)
        outs.append(_softmax_pv_t([(s, vT_ref[gs, :])], _sink_row(sink_ref, g, SEQ)))
    _store_heads(o_ref, outs, SEQ)


def _attn_ctx(qT, kv, vT, sink):
    return pl.pallas_call(
        _attn_ctx_kernel,
        grid=(BATCH,),
        in_specs=[
            pl.BlockSpec(memory_space=pltpu.SMEM),
            pl.BlockSpec((Q_DIM, SEQ), lambda b: (0, b)),
            pl.BlockSpec((SEQ, 2 * KV_DIM), lambda b: (b, 0)),
            pl.BlockSpec((KV_DIM, SEQ), lambda b: (0, b)),
        ],
        out_specs=pl.BlockSpec((SEQ, D), lambda b: (b, 0)),
        out_shape=jax.ShapeDtypeStruct((T_ALL, D), bf16),
        compiler_params=_params("parallel"),
        name="attn_ctx",
    )(sink, qT, kv, vT)


def _attn_lat_kernel(sink_ref, qT_ref, kp_ref, kc_ref, kn_ref, vp_ref, vc_ref, vn_ref, ck_ref, cv_ref, o_in_ref,
                     o_ref, cvT_scr):
    del o_in_ref
    qb = pl.program_id(1)

    @pl.when(qb == 0)
    def _():
        cvT_scr[...] = cv_ref[...].T.astype(bf16)

    n = Q_PER_KV * BLOCK
    row = lax.broadcasted_iota(jnp.int32, (WIN_KEYS, n), 0)
    ql = lax.broadcasted_iota(jnp.int32, (WIN_KEYS, n), 1) & (BLOCK - 1)
    kj = (qb - 1) * BLOCK + row
    rel = row - ql
    valid = (rel >= 0) & (rel <= 2 * WINDOW) & (kj >= 0) & (kj < DEC_SEQ)
    k_win = jnp.concatenate([kp_ref[...], kc_ref[...], kn_ref[...]], axis=0)
    k_ctx = ck_ref[...].astype(bf16)
    outs = []
    for g in range(KV_HEADS):
        gs = slice(g * HEAD_DIM, (g + 1) * HEAD_DIM)
        qg = _group_queries(qT_ref, g)
        s_loc = jnp.where(valid, jnp.dot(k_win[:, gs], qg, preferred_element_type=f32), NEG_INF)
        s_ctx = jnp.dot(k_ctx[:, gs], qg, preferred_element_type=f32)
        v_win = jnp.concatenate([vp_ref[gs, :], vc_ref[gs, :], vn_ref[gs, :]], axis=1)
        outs.append(_softmax_pv_t([(s_loc, v_win), (s_ctx, cvT_scr[gs, :])], _sink_row(sink_ref, g, BLOCK)))
    _store_heads(o_ref, outs, BLOCK)


def _attn_lat(qT, k, vT, o_ctx, cache_k, cache_v, sink, j):
    nqb = DEC_SEQ // BLOCK
    prev = lambda b, i: b * nqb + jnp.maximum(i - 1, 0)
    own = lambda b, i: b * nqb + i
    nxt = lambda b, i: b * nqb + jnp.minimum(i + 1, nqb - 1)
    return pl.pallas_call(
        _attn_lat_kernel,
        grid=(DEC_BATCH, nqb),
        in_specs=[
            pl.BlockSpec(memory_space=pltpu.SMEM),
            pl.BlockSpec((Q_DIM, BLOCK), lambda b, i: (0, own(b, i))),
            pl.BlockSpec((BLOCK, KV_DIM), lambda b, i: (prev(b, i), 0)),
            pl.BlockSpec((BLOCK, KV_DIM), lambda b, i: (own(b, i), 0)),
            pl.BlockSpec((BLOCK, KV_DIM), lambda b, i: (nxt(b, i), 0)),
            pl.BlockSpec((KV_DIM, BLOCK), lambda b, i: (0, prev(b, i))),
            pl.BlockSpec((KV_DIM, BLOCK), lambda b, i: (0, own(b, i))),
            pl.BlockSpec((KV_DIM, BLOCK), lambda b, i: (0, nxt(b, i))),
            pl.BlockSpec((None, None, SEQ, KV_DIM), lambda b, i: (b, j, 0, 0)),
            pl.BlockSpec((None, None, SEQ, KV_DIM), lambda b, i: (b, j, 0, 0)),
            pl.BlockSpec(memory_space=pl.ANY),
        ],
        out_specs=pl.BlockSpec((BLOCK, D), lambda b, i: (T_CTX // BLOCK + own(b, i), 0)),
        out_shape=jax.ShapeDtypeStruct((T_ALL, D), bf16),
        scratch_shapes=[pltpu.VMEM((KV_DIM, SEQ), bf16)],
        input_output_aliases={10: 0},
        compiler_params=_params("parallel", "arbitrary"),
        name="attn_lat",
    )(sink, qT, k, k, k, vT, vT, vT, cache_k, cache_v, o_ctx)


def _dft_tables(L):
    idx = np.arange(L)
    ft = np.outer(idx, idx) % (2 * L)
    cm = np.cos(np.pi * ft / L)
    sm = np.sin(np.pi * ft / L)
    nyq = np.where(idx % 2 == 0, 1.0, -1.0)
    sm_n = sm.copy()
    sm_n[0, :] = nyq
    fwd = np.concatenate([cm, sm_n], axis=0)
    wc = np.full((L,), 2.0)
    wc[0] = 1.0
    inv_c = cm.T * wc[None, :]
    inv_s = sm.T * 2.0
    inv_s[:, 0] = nyq
    inv = np.concatenate([inv_c, inv_s], axis=1) / (2 * L)
    return cm, sm, nyq, fwd, inv


def _filter_tables(L):
    t = np.arange(L, dtype=np.float64) / L
    bands = np.arange(1, FILTER_BANDS + 1, dtype=np.float64)
    ph = 2 * np.pi * t[:, None] * bands[None]
    feats = np.zeros((L, LANES))
    feats[:, :FILTER_EMB] = np.concatenate([t[:, None], np.sin(ph), np.cos(ph)], -1)
    max_decay = math.log(DECAY_TARGET) / DECAY_FAST_PCT
    min_decay = math.log(DECAY_TARGET) / DECAY_SLOW_PCT
    deltas = np.abs(np.linspace(min_decay, max_decay, D))
    decay = np.exp(-t[:, None] * deltas[None])
    return feats, decay


def _dot_hi(a, b):
    return jnp.dot(a, b, preferred_element_type=f32, precision=lax.Precision.HIGHEST)


def _split_bf16(a):
    hi = a.astype(bf16)
    return hi, (a - hi.astype(f32)).astype(bf16)


def _dot_split(t_hi, t_lo, b):
    b_hi, b_lo = _split_bf16(b)
    return (jnp.dot(t_hi, b_hi, preferred_element_type=f32) + jnp.dot(t_hi, b_lo, preferred_element_type=f32)
            + jnp.dot(t_lo, b_hi, preferred_element_type=f32))


def _filter_kernel(feat_ref, w1_ref, b1_ref, w2_ref, b2_ref, w3f_ref, w3b_ref, dec_ref, cmh_ref, cml_ref, smh_ref,
                   sml_ref, nyq_ref, o_ref):
    a = jnp.sin(_dot_hi(feat_ref[...], w1_ref[...]) + b1_ref[...])
    a = jnp.sin(_dot_hi(a, w2_ref[...]) + b2_ref[...])
    dec = dec_ref[...]
    fwd = _dot_hi(a, w3f_ref[...]) * dec
    bwd = _dot_hi(a, w3b_ref[...]) * dec
    row = lax.broadcasted_iota(jnp.int32, fwd.shape, 0)
    bwd = jnp.where(row == 0, 0.0, bwd)
    norm = jnp.sum(jnp.abs(fwd), axis=0, keepdims=True) + jnp.sum(jnp.abs(bwd), axis=0, keepdims=True) + 1e-6
    even = (fwd + bwd) / norm
    odd = (bwd - fwd) / norm
    hr = _dot_split(cmh_ref[...], cml_ref[...], even)
    hi = _dot_split(smh_ref[...], sml_ref[...], odd)
    h_nyq = jnp.sum(even * nyq_ref[...], axis=0, keepdims=True)
    o_ref[0] = hr
    o_ref[1] = jnp.where(row == 0, h_nyq, hi)


def _hyena_filter(L, j, w1p, b1p, w2p, b2p, w3p, name):
    cm, sm, nyq, _, _ = _dft_tables(L)
    feats, decay = _filter_tables(L)
    td = 512
    nd = D // td
    full = lambda shape: pl.BlockSpec(shape, lambda o, d: (0,) * len(shape))
    return pl.pallas_call(
        _filter_kernel,
        grid=(2, nd),
        in_specs=[
            full((L, LANES)),
            pl.BlockSpec((None, LANES, LANES), lambda o, d: (j, 0, 0)),
            pl.BlockSpec((None, 1, LANES), lambda o, d: (j, 0, 0)),
            pl.BlockSpec((None, LANES, LANES), lambda o, d: (j, 0, 0)),
            pl.BlockSpec((None, 1, LANES), lambda o, d: (j, 0, 0)),
            pl.BlockSpec((None, LANES, td), lambda o, d: (j, 0, (2 * o) * nd + d)),
            pl.BlockSpec((None, LANES, td), lambda o, d: (j, 0, (2 * o + 1) * nd + d)),
            pl.BlockSpec((L, td), lambda o, d: (0, d)),
            full((L, L)), full((L, L)), full((L, L)), full((L, L)),
            full((L, 1)),
        ],
        out_specs=pl.BlockSpec((2, L, td), lambda o, d: (o, 0, d)),
        out_shape=jax.ShapeDtypeStruct((4, L, D), f32),
        compiler_params=_params("parallel", "parallel"),
        name=name,
    )(jnp.asarray(feats, f32), w1p, b1p, w2p, b2p, w3p, w3p, jnp.asarray(decay, f32),
      *_np_split_bf16(cm), *_np_split_bf16(sm), jnp.asarray(nyq[:, None], f32))


def _np_split_bf16(t):
    hi = t.astype(bf16)
    lo = (t - hi.astype(np.float64)).astype(bf16)
    return jnp.asarray(hi), jnp.asarray(lo)


def _short_conv(u, w, b):
    L = u.shape[0]
    row = lax.broadcasted_iota(jnp.int32, u.shape, 0)
    prev = jnp.where(row == 0, 0.0, pltpu.roll(u, 1, 0))
    nxt = jnp.where(row == L - 1, 0.0, pltpu.roll(u, L - 1, 0))
    return prev * w[0:1, :] + u * w[1:2, :] + nxt * w[2:3, :] + b


def _long_conv(z, hr, hi, fwd, inv):
    L = z.shape[0]
    zs = jnp.dot(fwd, z.astype(bf16), preferred_element_type=f32)
    zr, zi = zs[:L], zs[L:]
    row = lax.broadcasted_iota(jnp.int32, hr.shape, 0)
    hi_m = jnp.where(row == 0, 0.0, hi)
    yr = zr * hr + zi * hi_m
    yi = zi * jnp.where(row == 0, hi, hr) - zr * hi_m
    ys = jnp.concatenate([yr, yi], axis=0).astype(bf16)
    return jnp.dot(inv, ys, preferred_element_type=f32)


def _hyena_conv_kernel(pv_ref, p1_ref, p2_ref, wv_ref, w1_ref, w2_ref, bv_ref, b1_ref, b2_ref, h_ref, d_ref,
                       fwd_ref, inv_ref, o_in_ref, o_ref):
    del o_in_ref
    L = h_ref.shape[1]
    fwd, inv = fwd_ref[...], inv_ref[...]
    for s in range(pv_ref.shape[0] // L):
        rows = slice(s * L, (s + 1) * L)
        v = _short_conv(pv_ref[rows, :], wv_ref[...], bv_ref[...])
        x1 = _short_conv(p1_ref[rows, :], w1_ref[...], b1_ref[...])
        x2 = _short_conv(p2_ref[rows, :], w2_ref[...], b2_ref[...])
        z = x1 * (_long_conv(v, h_ref[0], h_ref[1], fwd, inv) + v * d_ref[0:1, :])
        z = x2 * (_long_conv(z, h_ref[2], h_ref[3], fwd, inv) + z * d_ref[1:2, :])
        o_ref[rows, :] = z.astype(bf16)


def _hyena_conv(p, z_in, conv_w, conv_b, spec, hy_d, j, L, nb, nseq, row0, td, name):
    _, _, _, fwd, inv = _dft_tables(L)
    nd = D // td
    once = pl.Buffered(1)
    in_specs = []
    for c in range(3):
        in_specs.append(pl.BlockSpec((nseq * L, td), lambda d, b, c=c: (row0 // nseq + b, c * nd + d)))
    for c in range(3):
        in_specs.append(pl.BlockSpec((None, 3, td), lambda d, b, c=c: (j, 0, c * nd + d)))
    for c in range(3):
        in_specs.append(pl.BlockSpec((None, 1, td), lambda d, b, c=c: (j, 0, c * nd + d)))
    in_specs += [
        pl.BlockSpec((4, L, td), lambda d, b: (0, 0, d), pipeline_mode=once),
        pl.BlockSpec((None, 2, td), lambda d, b: (j, 0, d)),
        pl.BlockSpec((2 * L, L), lambda d, b: (0, 0), pipeline_mode=once),
        pl.BlockSpec((L, 2 * L), lambda d, b: (0, 0), pipeline_mode=once),
    ]
    args = [p, p, p, conv_w, conv_w, conv_w, conv_b, conv_b, conv_b, spec, hy_d,
            jnp.asarray(fwd, f32).astype(bf16), jnp.asarray(inv, f32).astype(bf16)]
    aliases = {}
    if z_in is not None:
        in_specs.append(pl.BlockSpec(memory_space=pl.ANY))
        args.append(z_in)
        aliases = {len(args) - 1: 0}
        kernel = _hyena_conv_kernel
    else:
        kernel = lambda *refs: _hyena_conv_kernel(*refs[:-1], None, refs[-1])
    return pl.pallas_call(
        kernel,
        grid=(nd, nb // nseq),
        in_specs=in_specs,
        out_specs=pl.BlockSpec((nseq * L, td), lambda d, b: (row0 // nseq + b, d)),
        out_shape=jax.ShapeDtypeStruct((T_ALL, D), bf16),
        input_output_aliases=aliases,
        compiler_params=_params("parallel", "parallel"),
        name=name,
    )(*args)


def _pad2(a, rows, cols):
    return jnp.pad(a, [(0, 0)] * (a.ndim - 2) + [(0, rows - a.shape[-2]), (0, cols - a.shape[-1])])


def kernel(x_prompt, x_sample, cache_k, cache_v, c, c_ctx, ada_w, ada_b, ln_g, ln_b, ffn_w1, ffn_w2, attn_w_qkv,
           attn_w_o, attn_sink, hy_w_in, hy_conv_w, hy_conv_b, hy_f_w1, hy_f_b1, hy_f_w2, hy_f_b2, hy_f_w3, hy_d,
           hy_w_out):
    x = jnp.concatenate([x_prompt.reshape(T_CTX, D), x_sample.reshape(T_LAT, D)], axis=0)
    cond = jnp.concatenate([c_ctx[None], c, jnp.zeros((N_COND - 1 - DEC_BATCH, D), f32)], axis=0)
    mod = _ada(cond, ada_w, ada_b)

    w1 = ffn_w1.astype(bf16)
    w2 = ffn_w2.astype(bf16)
    w_qkv = attn_w_qkv.astype(bf16)
    wqT = jnp.swapaxes(w_qkv[:, :, :Q_DIM], 1, 2)
    wkv = w_qkv[:, :, Q_DIM:]
    wvT = jnp.swapaxes(w_qkv[:, :, Q_DIM + KV_DIM:], 1, 2)
    w_o = attn_w_o.astype(bf16)
    w_in = hy_w_in.astype(bf16)
    w_out = hy_w_out.astype(bf16)
    ck = cache_k.reshape(DEC_BATCH, -1, SEQ, KV_DIM)
    cv = cache_v.reshape(DEC_BATCH, -1, SEQ, KV_DIM)
    rope = _rope_tables()
    fw1 = _pad2(hy_f_w1, LANES, LANES)
    fb1 = _pad2(hy_f_b1[:, None, :], 1, LANES)
    fw2 = _pad2(hy_f_w2, LANES, LANES)
    fb2 = _pad2(hy_f_b2[:, None, :], 1, LANES)
    fw3 = _pad2(hy_f_w3, LANES, hy_f_w3.shape[-1])
    conv_b = hy_conv_b[:, None, :]
    ln_g = ln_g.reshape(DEPTH * 3, 1, D)
    ln_b = ln_b.reshape(DEPTH * 3, 1, D)

    new_k, new_v = [], []
    for i in range(DEPTH):
        j = i // 2
        x = _ffn(x, mod, w1, w2, ln_g, ln_b, i, 0)
        if i % 2 == 0:
            qT_c, kv_c, vT_c = _qkv_ctx(x, mod, wqT, wkv, wvT, i, j)
            qT_l, k_l, vT_l = _qkv_lat(x, mod, wqT, wkv, wvT, rope, i, j)
            new_k.append(kv_c[:, :KV_DIM].reshape(BATCH, SEQ, KV_HEADS, HEAD_DIM))
            new_v.append(kv_c[:, KV_DIM:].reshape(BATCH, SEQ, KV_HEADS, HEAD_DIM))
            sink = attn_sink[j]
            z = _attn_ctx(qT_c, kv_c, vT_c, sink)
            z = _attn_lat(qT_l, k_l, vT_l, z, ck, cv, sink, j)
            x = _proj(x, z, mod, w_o, ln_g, ln_b, i, j, f"attn_out_l{i}")
        else:
            p = _modmm(x, mod, w_in, i, j, f"hy_in_l{i}")
            spec_ctx = _hyena_filter(SEQ, j, fw1, fb1, fw2, fb2, fw3, f"hy_filter_ctx_l{i}")
            spec_lat = _hyena_filter(DEC_SEQ, j, fw1, fb1, fw2, fb2, fw3, f"hy_filter_lat_l{i}")
            z = _hyena_conv(p, None, hy_conv_w, conv_b, spec_ctx, hy_d, j, SEQ, BATCH, 4, 0, 512,
                            f"hy_conv_ctx_l{i}")
            z = _hyena_conv(p, z, hy_conv_w, conv_b, spec_lat, hy_d, j, DEC_SEQ, DEC_BATCH, 2, T_CTX // DEC_SEQ, 256,
                            f"hy_conv_lat_l{i}")
            x = _proj(x, z, mod, w_out, ln_g, ln_b, i, j, f"hy_out_l{i}")
        x = _ffn(x, mod, w1, w2, ln_g, ln_b, i, 1)

    y_prompt = x[:T_CTX].reshape(BATCH, SEQ, D)
    y_sample = x[T_CTX:].reshape(DEC_BATCH, DEC_SEQ, D)
    return y_prompt, y_sample, jnp.stack(new_k, axis=1), jnp.stack(new_v, axis=1)
```

```python
import functools
import math

import numpy as np
import jax
import jax.numpy as jnp
from jax import lax
from jax.experimental import pallas as pl
from jax.experimental.pallas import tpu as pltpu

D = 1024
BATCH, SEQ = 16, 256
DEC_BATCH, DEC_SEQ = 8, 1024
DEPTH = 4
N_HEADS, HEAD_DIM, KV_HEADS = 16, 64, 4
Q_PER_KV = N_HEADS // KV_HEADS
Q_DIM = N_HEADS * HEAD_DIM
KV_DIM = KV_HEADS * HEAD_DIM
GRID_W = 64
WINDOW = 128
BLOCK = 128
WIN_KEYS = BLOCK + 2 * WINDOW
ROPE_BASE = 10000.0
ATTN_SCALE = HEAD_DIM ** -0.5
D_FF = 2816
N_MOD = 9
LN_EPS = 1e-5
ALPHA = (2 * DEPTH) ** 0.25
NEG_INF = -1e30
FILTER_BANDS = 16
FILTER_EMB = 1 + 2 * FILTER_BANDS
DECAY_FAST_PCT, DECAY_SLOW_PCT, DECAY_TARGET = 0.3, 1.5, 1e-2

T_CTX = BATCH * SEQ
T_LAT = DEC_BATCH * DEC_SEQ
T_ALL = T_CTX + T_LAT
N_COND = 16
TM = 1024
FFN_TM = 512
FFN_SUB = 256
LANES = 128
VMEM_LIMIT = 56 * 1024 * 1024

f32 = jnp.float32
bf16 = jnp.bfloat16


def _cond_of_tile(i, tm=TM):
    return jnp.where(i < T_CTX // tm, 0, 1 + (i * tm - T_CTX) // DEC_SEQ)


def _params(*sem):
    return pltpu.CompilerParams(dimension_semantics=sem, vmem_limit_bytes=VMEM_LIMIT)


def _layer_norm(y, g, b):
    mu = jnp.mean(y, axis=-1, keepdims=True)
    yc = y - mu
    var = jnp.mean(yc * yc, axis=-1, keepdims=True)
    return yc * lax.rsqrt(var + LN_EPS) * g + b


def _modulate(x_ref, mod_ref, base):
    shift = mod_ref[base:base + 1, :]
    scale = mod_ref[base + 1:base + 2, :]
    return (x_ref[...] * (1.0 + scale) + shift).astype(bf16)


def _dot_nt(a, b):
    return lax.dot_general(a, b, (((1,), (1,)), ((), ())), preferred_element_type=f32)


def _ada_kernel(c_ref, w_ref, b_ref, o_ref):
    s = jax.nn.silu(c_ref[...]).astype(bf16)
    o_ref[...] = jnp.dot(s, w_ref[...].astype(bf16), preferred_element_type=f32) + b_ref[...]


def _ada(cond, ada_w, ada_b):
    tn = 2304
    nt = (N_MOD * D) // tn
    out = pl.pallas_call(
        _ada_kernel,
        grid=(DEPTH, nt),
        in_specs=[
            pl.BlockSpec((N_COND, D), lambda l, n: (0, 0)),
            pl.BlockSpec((None, D, tn), lambda l, n: (l, 0, n)),
            pl.BlockSpec((None, 1, tn), lambda l, n: (l, 0, n)),
        ],
        out_specs=pl.BlockSpec((N_COND, tn), lambda l, n: (0, l * nt + n)),
        out_shape=jax.ShapeDtypeStruct((N_COND, DEPTH * N_MOD * D), f32),
        compiler_params=_params("arbitrary", "arbitrary"),
        name="ada_mod",
    )(cond, ada_w, ada_b.reshape(DEPTH, 1, N_MOD * D))
    return out.reshape(N_COND, DEPTH, N_MOD, D)


N_CTX_TILES = T_CTX // FFN_TM
N_TILES = T_ALL // FFN_TM


def _ffn_kernel(*refs, mod_base, tile0, split_x, mixer):
    refs = list(refs)
    o_ref = refs.pop()
    x_refs = [refs.pop(0) for _ in range(2 if split_x else 1)]
    mod_ref = refs.pop(0)
    if mixer:
        zc_ref, zl_ref, wp_ref, gp_ref, bp_ref = [refs.pop(0) for _ in range(5)]
    w1_ref, w2_ref, g_ref, b_ref = refs
    is_ctx = tile0 + pl.program_id(0) < N_CTX_TILES
    shift = mod_ref[mod_base:mod_base + 1, :]
    scale = mod_ref[mod_base + 1:mod_base + 2, :]
    half_gate = 0.5 * mod_ref[mod_base + 2:mod_base + 3, :]
    for r in range(FFN_TM // FFN_SUB):
        rows = slice(r * FFN_SUB, (r + 1) * FFN_SUB)
        x = x_refs[0][rows, :]
        if split_x:
            x = jnp.where(is_ctx, x, x_refs[1][rows, :])
        if mixer:
            z = jnp.where(is_ctx, zc_ref[rows, :], zl_ref[rows, :])
            f = jnp.dot(z, wp_ref[...], preferred_element_type=f32)
            x = _layer_norm(ALPHA * x + mod_ref[5:6, :] * f, gp_ref[...], bp_ref[...])
        h = (x * (1.0 + scale) + shift).astype(bf16)
        gu = jnp.dot(h, w1_ref[...], preferred_element_type=f32)
        a = (jax.nn.silu(gu[:, :D_FF]) * gu[:, D_FF:]).astype(bf16)
        f = jnp.dot(a, w2_ref[...], preferred_element_type=f32)
        o_ref[rows, :] = _layer_norm(ALPHA * x + half_gate * f, g_ref[...], b_ref[...])


def _ffn(xs, mod, w1, w2, ln_g, ln_b, layer, half, mixer=None, tiles=(0, N_TILES), name=None):
    tile0, tile1 = tiles
    resident = pl.Buffered(1)
    ctx_tile = lambda i: (jnp.minimum(tile0 + i, N_CTX_TILES - 1), 0)
    lat_tile = lambda i: (jnp.maximum(tile0 + i - N_CTX_TILES, 0), 0)
    row = lambda k: pl.BlockSpec((None, 1, D), lambda i: (3 * layer + k, 0, 0))
    if len(xs) == 2:
        in_specs = [pl.BlockSpec((FFN_TM, D), ctx_tile), pl.BlockSpec((FFN_TM, D), lat_tile)]
    else:
        in_specs = [pl.BlockSpec((FFN_TM, D), lambda i: (tile0 + i, 0))]
    args = list(xs)
    in_specs.append(pl.BlockSpec((None, None, N_MOD, D), lambda i: (_cond_of_tile(tile0 + i, FFN_TM), layer, 0, 0)))
    args.append(mod)
    if mixer is not None:
        zc, zl, wp, j = mixer
        in_specs += [pl.BlockSpec((FFN_TM, D), ctx_tile), pl.BlockSpec((FFN_TM, D), lat_tile),
                     pl.BlockSpec((None, D, D), lambda i: (j, 0, 0), pipeline_mode=resident), row(1), row(1)]
        args += [zc, zl, wp, ln_g, ln_b]
    in_specs += [
        pl.BlockSpec((None, None, D, 2 * D_FF), lambda i: (layer, half, 0, 0), pipeline_mode=resident),
        pl.BlockSpec((None, None, D_FF, D), lambda i: (layer, half, 0, 0), pipeline_mode=resident),
        row(2 * half), row(2 * half),
    ]
    args += [w1, w2, ln_g, ln_b]
    return pl.pallas_call(
        functools.partial(_ffn_kernel, mod_base=6 * half, tile0=tile0, split_x=len(xs) == 2,
                          mixer=mixer is not None),
        grid=(tile1 - tile0,),
        in_specs=in_specs,
        out_specs=pl.BlockSpec((FFN_TM, D), lambda i: (i, 0)),
        out_shape=jax.ShapeDtypeStruct(((tile1 - tile0) * FFN_TM, D), f32),
        compiler_params=_params("parallel"),
        name=name or f"ffn_l{layer}h{half}",
    )(*args)


def _modmm_kernel(x_ref, mod_ref, w_ref, o_ref):
    o_ref[...] = jnp.dot(_modulate(x_ref, mod_ref, 3), w_ref[...], preferred_element_type=f32)


def _modmm(x, mod, w, layer, j, name):
    n = w.shape[-1]
    tn = 1536
    return pl.pallas_call(
        _modmm_kernel,
        grid=(T_ALL // TM, n // tn),
        in_specs=[
            pl.BlockSpec((TM, D), lambda i, c: (i, 0)),
            pl.BlockSpec((None, None, N_MOD, D), lambda i, c: (_cond_of_tile(i), layer, 0, 0)),
            pl.BlockSpec((None, D, tn), lambda i, c: (j, 0, c)),
        ],
        out_specs=pl.BlockSpec((TM, tn), lambda i, c: (i, c)),
        out_shape=jax.ShapeDtypeStruct((T_ALL, n), f32),
        compiler_params=_params("parallel", "arbitrary"),
        name=name,
    )(x, mod, w)


def _rope_tables():
    t = np.arange(DEC_SEQ)
    pos = np.stack([t // GRID_W, t % GRID_W], axis=1).astype(np.float64)
    lane = np.arange(LANES)
    hl = lane % HEAD_DIM
    axis = hl // (HEAD_DIM // 2)
    half = HEAD_DIM // 4
    inv = ROPE_BASE ** (-(hl % half).astype(np.float64) / half)
    ang = pos[:, axis] * inv[None, :]
    first = (hl % (HEAD_DIM // 2)) < half
    cos = np.cos(ang)
    sin = np.where(first[None, :], -np.sin(ang), np.sin(ang))
    return (jnp.asarray(cos, f32), jnp.asarray(sin, f32),
            jnp.asarray(cos[:, :HEAD_DIM].T, f32), jnp.asarray(sin[:, :HEAD_DIM].T, f32))


def _rope_rows(x, cos, sin):
    lane = lax.broadcasted_iota(jnp.int32, (x.shape[0], LANES), 1)
    first = (lane % (HEAD_DIM // 2)) < (HEAD_DIM // 4)
    cols = []
    for c in range(x.shape[1] // LANES):
        xc = x[:, c * LANES:(c + 1) * LANES]
        partner = jnp.where(first, pltpu.roll(xc, LANES - HEAD_DIM // 4, 1), pltpu.roll(xc, HEAD_DIM // 4, 1))
        cols.append(xc * cos + partner * sin)
    return jnp.concatenate(cols, axis=1)


def _rope_cols(xT, cosT, sinT):
    q = HEAD_DIM // 4
    heads = []
    for h in range(xT.shape[0] // HEAD_DIM):
        xh = xT[h * HEAD_DIM:(h + 1) * HEAD_DIM, :]
        partner = jnp.concatenate([xh[q:2 * q], xh[0:q], xh[3 * q:4 * q], xh[2 * q:3 * q]], axis=0)
        heads.append(xh * cosT + partner * sinT)
    return jnp.concatenate(heads, axis=0)


def _qkv_ctx_kernel(x_ref, mod_ref, wqT_ref, wkv_ref, wvT_ref, qT_ref, kv_ref, vT_ref):
    h = _modulate(x_ref, mod_ref, 3)
    qT_ref[...] = (_dot_nt(wqT_ref[...], h) * ATTN_SCALE).astype(bf16)
    kv_ref[...] = jnp.dot(h, wkv_ref[...], preferred_element_type=f32)
    vT_ref[...] = _dot_nt(wvT_ref[...], h).astype(bf16)


def _qkv_ctx(x, mod, wqT, wkv, wvT, layer, j):
    return pl.pallas_call(
        _qkv_ctx_kernel,
        grid=(T_CTX // TM,),
        in_specs=[
            pl.BlockSpec((TM, D), lambda i: (i, 0)),
            pl.BlockSpec((None, None, N_MOD, D), lambda i: (0, layer, 0, 0)),
            pl.BlockSpec((None, Q_DIM, D), lambda i: (j, 0, 0)),
            pl.BlockSpec((None, D, 2 * KV_DIM), lambda i: (j, 0, 0)),
            pl.BlockSpec((None, KV_DIM, D), lambda i: (j, 0, 0)),
        ],
        out_specs=[
            pl.BlockSpec((Q_DIM, TM), lambda i: (0, i)),
            pl.BlockSpec((TM, 2 * KV_DIM), lambda i: (i, 0)),
            pl.BlockSpec((KV_DIM, TM), lambda i: (0, i)),
        ],
        out_shape=[
            jax.ShapeDtypeStruct((Q_DIM, T_CTX), bf16),
            jax.ShapeDtypeStruct((T_CTX, 2 * KV_DIM), f32),
            jax.ShapeDtypeStruct((KV_DIM, T_CTX), bf16),
        ],
        compiler_params=_params("parallel"),
        name=f"qkv_ctx_l{layer}",
    )(x, mod, wqT, wkv, wvT)


def _qkv_lat_kernel(x_ref, mod_ref, wqT_ref, wkv_ref, wvT_ref, cos_ref, sin_ref, cosT_ref, sinT_ref,
                    qT_ref, k_ref, vT_ref):
    h = _modulate(x_ref, mod_ref, 3)
    qT = _rope_cols(_dot_nt(wqT_ref[...], h), cosT_ref[...], sinT_ref[...])
    qT_ref[...] = (qT * ATTN_SCALE).astype(bf16)
    k = jnp.dot(h, wkv_ref[:, :KV_DIM], preferred_element_type=f32)
    k_ref[...] = _rope_rows(k, cos_ref[...], sin_ref[...]).astype(bf16)
    vT_ref[...] = _dot_nt(wvT_ref[...], h).astype(bf16)


def _qkv_lat(x, mod, wqT, wkv, wvT, rope, layer, j):
    row0 = T_CTX // DEC_SEQ
    full = lambda shape: pl.BlockSpec(shape, lambda b: (0, 0))
    return pl.pallas_call(
        _qkv_lat_kernel,
        grid=(DEC_BATCH,),
        in_specs=[
            pl.BlockSpec((DEC_SEQ, D), lambda b: (row0 + b, 0)),
            pl.BlockSpec((None, None, N_MOD, D), lambda b: (1 + b, layer, 0, 0)),
            pl.BlockSpec((None, Q_DIM, D), lambda b: (j, 0, 0)),
            pl.BlockSpec((None, D, 2 * KV_DIM), lambda b: (j, 0, 0)),
            pl.BlockSpec((None, KV_DIM, D), lambda b: (j, 0, 0)),
            full((DEC_SEQ, LANES)), full((DEC_SEQ, LANES)), full((HEAD_DIM, DEC_SEQ)), full((HEAD_DIM, DEC_SEQ)),
        ],
        out_specs=[
            pl.BlockSpec((Q_DIM, DEC_SEQ), lambda b: (0, b)),
            pl.BlockSpec((DEC_SEQ, KV_DIM), lambda b: (b, 0)),
            pl.BlockSpec((KV_DIM, DEC_SEQ), lambda b: (0, b)),
        ],
        out_shape=[
            jax.ShapeDtypeStruct((Q_DIM, T_LAT), bf16),
            jax.ShapeDtypeStruct((T_LAT, KV_DIM), bf16),
            jax.ShapeDtypeStruct((KV_DIM, T_LAT), bf16),
        ],
        compiler_params=_params("parallel"),
        name=f"qkv_lat_l{layer}",
    )(x, mod, wqT, wkv, wvT, *rope)


def _softmax_pv_t(parts, sink_row):
    m = sink_row
    for s, _ in parts:
        m = jnp.maximum(m, jnp.max(s, axis=0, keepdims=True))
    denom = jnp.exp(sink_row - m)
    o = None
    for s, vT in parts:
        p = jnp.exp(s - m)
        denom = denom + jnp.sum(p, axis=0, keepdims=True)
        pv = jnp.dot(vT, p.astype(bf16), preferred_element_type=f32)
        o = pv if o is None else o + pv
    return o / denom


def _group_queries(qT_ref, g):
    heads = [qT_ref[(Q_PER_KV * g + i) * HEAD_DIM:(Q_PER_KV * g + i + 1) * HEAD_DIM, :] for i in range(Q_PER_KV)]
    return jnp.concatenate(heads, axis=1)


def _sink_row(sink_ref, g, nq):
    return jnp.concatenate([jnp.full((1, nq), sink_ref[Q_PER_KV * g + i], f32) for i in range(Q_PER_KV)], axis=1)


def _store_heads(o_ref, group_outs, nq):
    heads = [oT[:, i * nq:(i + 1) * nq] for oT in group_outs for i in range(Q_PER_KV)]
    o_ref[...] = jnp.concatenate(heads, axis=0).T.astype(o_ref.dtype)


def _attn_ctx_kernel(sink_ref, qT_ref, kv_ref, vT_ref, o_ref):
    k = kv_ref[:, :KV_DIM].astype(bf16)
    outs = []
    for g in range(KV_HEADS):
        gs = slice(g * HEAD_DIM, (g + 1) * HEAD_DIM)
        s = jnp.dot(k[:, gs], _group_queries(qT_ref, g), preferred_element_type=f32)
        outs.append(_softmax_pv_t([(s, vT_ref[gs, :])], _sink_row(sink_ref, g, SEQ)))
    _store_heads(o_ref, outs, SEQ)


def _attn_ctx(qT, kv, vT, sink):
    return pl.pallas_call(
        _attn_ctx_kernel,
        grid=(BATCH,),
        in_specs=[
            pl.BlockSpec(memory_space=pltpu.SMEM),
            pl.BlockSpec((Q_DIM, SEQ), lambda b: (0, b)),
            pl.BlockSpec((SEQ, 2 * KV_DIM), lambda b: (b, 0)),
            pl.BlockSpec((KV_DIM, SEQ), lambda b: (0, b)),
        ],
        out_specs=pl.BlockSpec((SEQ, D), lambda b: (b, 0)),
        out_shape=jax.ShapeDtypeStruct((T_CTX, D), bf16),
        compiler_params=_params("parallel"),
        name="attn_ctx",
    )(sink, qT, kv, vT)


def _attn_lat_kernel(sink_ref, qT_ref, kp_ref, kc_ref, kn_ref, vp_ref, vc_ref, vn_ref, ck_ref, cv_ref, o_ref,
                     cvT_scr):
    qb = pl.program_id(1)

    @pl.when(qb == 0)
    def _():
        cvT_scr[...] = cv_ref[...].T.astype(bf16)

    n = Q_PER_KV * BLOCK
    row = lax.broadcasted_iota(jnp.int32, (WIN_KEYS, n), 0)
    ql = lax.broadcasted_iota(jnp.int32, (WIN_KEYS, n), 1) & (BLOCK - 1)
    kj = (qb - 1) * BLOCK + row
    rel = row - ql
    valid = (rel >= 0) & (rel <= 2 * WINDOW) & (kj >= 0) & (kj < DEC_SEQ)
    k_win = jnp.concatenate([kp_ref[...], kc_ref[...], kn_ref[...]], axis=0)
    k_ctx = ck_ref[...].astype(bf16)
    outs = []
    for g in range(KV_HEADS):
        gs = slice(g * HEAD_DIM, (g + 1) * HEAD_DIM)
        qg = _group_queries(qT_ref, g)
        s_loc = jnp.where(valid, jnp.dot(k_win[:, gs], qg, preferred_element_type=f32), NEG_INF)
        s_ctx = jnp.dot(k_ctx[:, gs], qg, preferred_element_type=f32)
        v_win = jnp.concatenate([vp_ref[gs, :], vc_ref[gs, :], vn_ref[gs, :]], axis=1)
        outs.append(_softmax_pv_t([(s_loc, v_win), (s_ctx, cvT_scr[gs, :])], _sink_row(sink_ref, g, BLOCK)))
    _store_heads(o_ref, outs, BLOCK)


def _attn_lat(qT, k, vT, cache_k, cache_v, sink, j):
    nqb = DEC_SEQ // BLOCK
    prev = lambda b, i: b * nqb + jnp.maximum(i - 1, 0)
    own = lambda b, i: b * nqb + i
    nxt = lambda b, i: b * nqb + jnp.minimum(i + 1, nqb - 1)
    return pl.pallas_call(
        _attn_lat_kernel,
        grid=(DEC_BATCH, nqb),
        in_specs=[
            pl.BlockSpec(memory_space=pltpu.SMEM),
            pl.BlockSpec((Q_DIM, BLOCK), lambda b, i: (0, own(b, i))),
            pl.BlockSpec((BLOCK, KV_DIM), lambda b, i: (prev(b, i), 0)),
            pl.BlockSpec((BLOCK, KV_DIM), lambda b, i: (own(b, i), 0)),
            pl.BlockSpec((BLOCK, KV_DIM), lambda b, i: (nxt(b, i), 0)),
            pl.BlockSpec((KV_DIM, BLOCK), lambda b, i: (0, prev(b, i))),
            pl.BlockSpec((KV_DIM, BLOCK), lambda b, i: (0, own(b, i))),
            pl.BlockSpec((KV_DIM, BLOCK), lambda b, i: (0, nxt(b, i))),
            pl.BlockSpec((None, None, SEQ, KV_DIM), lambda b, i: (b, j, 0, 0)),
            pl.BlockSpec((None, None, SEQ, KV_DIM), lambda b, i: (b, j, 0, 0)),
        ],
        out_specs=pl.BlockSpec((BLOCK, D), lambda b, i: (own(b, i), 0)),
        out_shape=jax.ShapeDtypeStruct((T_LAT, D), bf16),
        scratch_shapes=[pltpu.VMEM((KV_DIM, SEQ), bf16)],
        compiler_params=_params("parallel", "arbitrary"),
        name="attn_lat",
    )(sink, qT, k, k, k, vT, vT, vT, cache_k, cache_v)


def _dft_tables(L):
    idx = np.arange(L)
    ft = np.outer(idx, idx) % (2 * L)
    cm = np.cos(np.pi * ft / L)
    sm = np.sin(np.pi * ft / L)
    nyq = np.where(idx % 2 == 0, 1.0, -1.0)
    sm_n = sm.copy()
    sm_n[0, :] = nyq
    fwd = np.concatenate([cm, sm_n], axis=0)
    wc = np.full((L,), 2.0)
    wc[0] = 1.0
    inv_c = cm.T * wc[None, :]
    inv_s = sm.T * 2.0
    inv_s[:, 0] = nyq
    inv = np.concatenate([inv_c, inv_s], axis=1) / (2 * L)
    return cm, sm, nyq, fwd, inv


def _filter_tables(L):
    t = np.arange(L, dtype=np.float64) / L
    bands = np.arange(1, FILTER_BANDS + 1, dtype=np.float64)
    ph = 2 * np.pi * t[:, None] * bands[None]
    feats = np.zeros((L, LANES))
    feats[:, :FILTER_EMB] = np.concatenate([t[:, None], np.sin(ph), np.cos(ph)], -1)
    max_decay = math.log(DECAY_TARGET) / DECAY_FAST_PCT
    min_decay = math.log(DECAY_TARGET) / DECAY_SLOW_PCT
    deltas = np.abs(np.linspace(min_decay, max_decay, D))
    decay = np.exp(-t[:, None] * deltas[None])
    return feats, decay


def _dot_hi(a, b):
    return jnp.dot(a, b, preferred_element_type=f32, precision=lax.Precision.HIGHEST)


def _split_bf16(a):
    hi = a.astype(bf16)
    return hi, (a - hi.astype(f32)).astype(bf16)


def _dot_split(t_hi, t_lo, b):
    b_hi, b_lo = _split_bf16(b)
    return (jnp.dot(t_hi, b_hi, preferred_element_type=f32) + jnp.dot(t_hi, b_lo, preferred_element_type=f32)
            + jnp.dot(t_lo, b_hi, preferred_element_type=f32))


def _filter_kernel(feat_ref, w1_ref, b1_ref, w2_ref, b2_ref, w3f_ref, w3b_ref, dec_ref, cmh_ref, cml_ref, smh_ref,
                   sml_ref, nyq_ref, o_ref):
    a = jnp.sin(_dot_hi(feat_ref[...], w1_ref[...]) + b1_ref[...])
    a = jnp.sin(_dot_hi(a, w2_ref[...]) + b2_ref[...])
    dec = dec_ref[...]
    fwd = _dot_hi(a, w3f_ref[...]) * dec
    bwd = _dot_hi(a, w3b_ref[...]) * dec
    row = lax.broadcasted_iota(jnp.int32, fwd.shape, 0)
    bwd = jnp.where(row == 0, 0.0, bwd)
    norm = jnp.sum(jnp.abs(fwd), axis=0, keepdims=True) + jnp.sum(jnp.abs(bwd), axis=0, keepdims=True) + 1e-6
    even = (fwd + bwd) / norm
    odd = (bwd - fwd) / norm
    hr = _dot_split(cmh_ref[...], cml_ref[...], even)
    hi = _dot_split(smh_ref[...], sml_ref[...], odd)
    h_nyq = jnp.sum(even * nyq_ref[...], axis=0, keepdims=True)
    o_ref[0] = hr
    o_ref[1] = jnp.where(row == 0, 0.0, hi)
    o_ref[2] = jnp.where(row == 0, h_nyq, hr)


def _hyena_filter(L, j, w1p, b1p, w2p, b2p, w3p, name):
    cm, sm, nyq, _, _ = _dft_tables(L)
    feats, decay = _filter_tables(L)
    td = 512
    nd = D // td
    full = lambda shape: pl.BlockSpec(shape, lambda o, d: (0,) * len(shape))
    return pl.pallas_call(
        _filter_kernel,
        grid=(2, nd),
        in_specs=[
            full((L, LANES)),
            pl.BlockSpec((None, LANES, LANES), lambda o, d: (j, 0, 0)),
            pl.BlockSpec((None, 1, LANES), lambda o, d: (j, 0, 0)),
            pl.BlockSpec((None, LANES, LANES), lambda o, d: (j, 0, 0)),
            pl.BlockSpec((None, 1, LANES), lambda o, d: (j, 0, 0)),
            pl.BlockSpec((None, LANES, td), lambda o, d: (j, 0, (2 * o) * nd + d)),
            pl.BlockSpec((None, LANES, td), lambda o, d: (j, 0, (2 * o + 1) * nd + d)),
            pl.BlockSpec((L, td), lambda o, d: (0, d)),
            full((L, L)), full((L, L)), full((L, L)), full((L, L)),
            full((L, 1)),
        ],
        out_specs=pl.BlockSpec((3, L, td), lambda o, d: (o, 0, d)),
        out_shape=jax.ShapeDtypeStruct((6, L, D), f32),
        compiler_params=_params("parallel", "parallel"),
        name=name,
    )(jnp.asarray(feats, f32), w1p, b1p, w2p, b2p, w3p, w3p, jnp.asarray(decay, f32),
      *_np_split_bf16(cm), *_np_split_bf16(sm), jnp.asarray(nyq[:, None], f32))


def _np_split_bf16(t):
    hi = t.astype(bf16)
    lo = (t - hi.astype(np.float64)).astype(bf16)
    return jnp.asarray(hi), jnp.asarray(lo)


def _short_conv(u, w, b):
    L = u.shape[0]
    row = lax.broadcasted_iota(jnp.int32, u.shape, 0)
    prev = jnp.where(row == 0, 0.0, pltpu.roll(u, 1, 0))
    nxt = jnp.where(row == L - 1, 0.0, pltpu.roll(u, L - 1, 0))
    return prev * w[0:1, :] + u * w[1:2, :] + nxt * w[2:3, :] + b


def _spectral_product(zs, h_ref, o):
    L = zs.shape[0] // 2
    zr, zi = zs[:L], zs[L:]
    hr, hi_m, hr_n = h_ref[3 * o], h_ref[3 * o + 1], h_ref[3 * o + 2]
    return jnp.concatenate([zr * hr + zi * hi_m, zi * hr_n - zr * hi_m], axis=0).astype(bf16)


def _hyena_conv_kernel(pv_ref, p1_ref, p2_ref, wv_ref, w1_ref, w2_ref, bv_ref, b1_ref, b2_ref, h_ref, d_ref,
                       fwd_ref, inv_ref, o_ref):
    L = h_ref.shape[1]
    seqs = [slice(s * L, (s + 1) * L) for s in range(pv_ref.shape[0] // L)]
    dft = lambda z: jnp.dot(fwd_ref[...], z.astype(bf16), preferred_element_type=f32)
    idft = lambda y: jnp.dot(inv_ref[...], y, preferred_element_type=f32)
    z = [_short_conv(pv_ref[r, :], wv_ref[...], bv_ref[...]) for r in seqs]
    gates = [[_short_conv(p1_ref[r, :], w1_ref[...], b1_ref[...]) for r in seqs],
             [_short_conv(p2_ref[r, :], w2_ref[...], b2_ref[...]) for r in seqs]]
    for o in range(2):
        zs = [dft(zz) for zz in z]
        ys = [_spectral_product(s, h_ref, o) for s in zs]
        y = [idft(s) for s in ys]
        z = [g * (yy + zz * d_ref[o:o + 1, :]) for g, yy, zz in zip(gates[o], y, z)]
    for r, zz in zip(seqs, z):
        o_ref[r, :] = zz.astype(bf16)


def _hyena_conv(p, conv_w, conv_b, spec, hy_d, j, L, nb, nseq, row0, td, name):
    _, _, _, fwd, inv = _dft_tables(L)
    nd = D // td
    once = pl.Buffered(1)
    in_specs = []
    for c in range(3):
        in_specs.append(pl.BlockSpec((nseq * L, td), lambda d, b, c=c: (row0 // nseq + b, c * nd + d)))
    for c in range(3):
        in_specs.append(pl.BlockSpec((None, 3, td), lambda d, b, c=c: (j, 0, c * nd + d)))
    for c in range(3):
        in_specs.append(pl.BlockSpec((None, 1, td), lambda d, b, c=c: (j, 0, c * nd + d)))
    in_specs += [
        pl.BlockSpec((6, L, td), lambda d, b: (0, 0, d), pipeline_mode=once),
        pl.BlockSpec((None, 2, td), lambda d, b: (j, 0, d)),
        pl.BlockSpec((2 * L, L), lambda d, b: (0, 0), pipeline_mode=once),
        pl.BlockSpec((L, 2 * L), lambda d, b: (0, 0), pipeline_mode=once),
    ]
    return pl.pallas_call(
        _hyena_conv_kernel,
        grid=(nd, nb // nseq),
        in_specs=in_specs,
        out_specs=pl.BlockSpec((nseq * L, td), lambda d, b: (b, d)),
        out_shape=jax.ShapeDtypeStruct((nb * L, D), bf16),
        compiler_params=_params("parallel", "parallel"),
        name=name,
    )(p, p, p, conv_w, conv_w, conv_w, conv_b, conv_b, conv_b, spec, hy_d,
      jnp.asarray(fwd, f32).astype(bf16), jnp.asarray(inv, f32).astype(bf16))


def _pad2(a, rows, cols):
    return jnp.pad(a, [(0, 0)] * (a.ndim - 2) + [(0, rows - a.shape[-2]), (0, cols - a.shape[-1])])


def kernel(x_prompt, x_sample, cache_k, cache_v, c, c_ctx, ada_w, ada_b, ln_g, ln_b, ffn_w1, ffn_w2, attn_w_qkv,
           attn_w_o, attn_sink, hy_w_in, hy_conv_w, hy_conv_b, hy_f_w1, hy_f_b1, hy_f_w2, hy_f_b2, hy_f_w3, hy_d,
           hy_w_out):
    cond =jnp.concatenate([c_ctx[None], c, jnp.zeros((N_COND - 1 - DEC_BATCH, D), f32)], axis=0)
    mod = _ada(cond, ada_w, ada_b)

    w1 = ffn_w1.astype(bf16)
    w2 = ffn_w2.astype(bf16)
    w_qkv = attn_w_qkv.astype(bf16)
    wqT = jnp.swapaxes(w_qkv[:, :, :Q_DIM], 1, 2)
    wkv = w_qkv[:, :, Q_DIM:]
    wvT = jnp.swapaxes(w_qkv[:, :, Q_DIM + KV_DIM:], 1, 2)
    w_o = attn_w_o.astype(bf16)
    w_in = hy_w_in.astype(bf16)
    w_out = hy_w_out.astype(bf16)
    ck = cache_k.reshape(DEC_BATCH, -1, SEQ, KV_DIM)
    cv = cache_v.reshape(DEC_BATCH, -1, SEQ, KV_DIM)
    rope = _rope_tables()
    fw1 = _pad2(hy_f_w1, LANES, LANES)
    fb1 = _pad2(hy_f_b1[:, None, :], 1, LANES)
    fw2 = _pad2(hy_f_w2, LANES, LANES)
    fb2 = _pad2(hy_f_b2[:, None, :], 1, LANES)
    fw3 = _pad2(hy_f_w3, LANES, hy_f_w3.shape[-1])
    conv_b = hy_conv_b[:, None, :]
    ln_g = ln_g.reshape(DEPTH * 3, 1, D)
    ln_b = ln_b.reshape(DEPTH * 3, 1, D)

    new_k, new_v = [], []
    xs = (x_prompt.reshape(T_CTX, D), x_sample.reshape(T_LAT, D))
    for i in range(DEPTH):
        j = i // 2
        x = _ffn(xs, mod, w1, w2, ln_g, ln_b, i, 0)
        if i % 2 == 0:
            qT_c, kv_c, vT_c = _qkv_ctx(x, mod, wqT, wkv, wvT, i, j)
            qT_l, k_l, vT_l = _qkv_lat(x, mod, wqT, wkv, wvT, rope, i, j)
            new_k.append(kv_c[:, :KV_DIM].reshape(BATCH, SEQ, KV_HEADS, HEAD_DIM))
            new_v.append(kv_c[:, KV_DIM:].reshape(BATCH, SEQ, KV_HEADS, HEAD_DIM))
            sink = attn_sink[j]
            mixer = (_attn_ctx(qT_c, kv_c, vT_c, sink), _attn_lat(qT_l, k_l, vT_l, ck, cv, sink, j), w_o, j)
        else:
            p = _modmm(x, mod, w_in, i, j, f"hy_in_l{i}")
            spec_ctx = _hyena_filter(SEQ, j, fw1, fb1, fw2, fb2, fw3, f"hy_filter_ctx_l{i}")
            spec_lat = _hyena_filter(DEC_SEQ, j, fw1, fb1, fw2, fb2, fw3, f"hy_filter_lat_l{i}")
            z_c = _hyena_conv(p, hy_conv_w, conv_b, spec_ctx, hy_d, j, SEQ, BATCH, 4, 0, 512, f"hy_conv_ctx_l{i}")
            z_l = _hyena_conv(p, hy_conv_w, conv_b, spec_lat, hy_d, j, DEC_SEQ, DEC_BATCH, 2, T_CTX // DEC_SEQ, 256,
                              f"hy_conv_lat_l{i}")
            mixer = (z_c, z_l, w_out, j)
        if i < DEPTH - 1:
            xs = (_ffn((x,), mod, w1, w2, ln_g, ln_b, i, 1, mixer=mixer),)
    y_prompt = _ffn((x,), mod, w1, w2, ln_g, ln_b, DEPTH - 1, 1, mixer=mixer, tiles=(0, N_CTX_TILES),
                    name="ffn_last_ctx")
    y_sample = _ffn((x,), mod, w1, w2, ln_g, ln_b, DEPTH - 1, 1, mixer=mixer, tiles=(N_CTX_TILES, N_TILES),
                    name="ffn_last_lat")
    return (y_prompt.reshape(BATCH, SEQ, D), y_sample.reshape(DEC_BATCH, DEC_SEQ, D),
            jnp.stack(new_k, axis=1), jnp.stack(new_v, axis=1))
```

```python
import functools
import math

import numpy as np
import jax
import jax.numpy as jnp
from jax import lax
from jax.experimental import pallas as pl
from jax.experimental.pallas import tpu as pltpu

D = 1024
BATCH, SEQ = 16, 256
DEC_BATCH, DEC_SEQ = 8, 1024
DEPTH = 4
N_HEADS, HEAD_DIM, KV_HEADS = 16, 64, 4
Q_PER_KV = N_HEADS // KV_HEADS
Q_DIM = N_HEADS * HEAD_DIM
KV_DIM = KV_HEADS * HEAD_DIM
GRID_W = 64
WINDOW = 128
BLOCK = 128
WIN_KEYS = BLOCK + 2 * WINDOW
ROPE_BASE = 10000.0
ATTN_SCALE = HEAD_DIM ** -0.5
D_FF = 2816
N_MOD = 9
LN_EPS = 1e-5
ALPHA = (2 * DEPTH) ** 0.25
NEG_INF = -1e30
FILTER_BANDS = 16
FILTER_EMB = 1 + 2 * FILTER_BANDS
DECAY_FAST_PCT, DECAY_SLOW_PCT, DECAY_TARGET = 0.3, 1.5, 1e-2

T_CTX = BATCH * SEQ
T_LAT = DEC_BATCH * DEC_SEQ
T_ALL = T_CTX + T_LAT
N_COND = 16
TM = 1024
FFN_TM = 1024
FFN_SUB = 256
LANES = 128
VMEM_LIMIT = 56 * 1024 * 1024

f32 = jnp.float32
bf16 = jnp.bfloat16


def _cond_of_tile(i, tm=TM):
    return jnp.where(i < T_CTX // tm, 0, 1 + (i * tm - T_CTX) // DEC_SEQ)


def _params(*sem):
    return pltpu.CompilerParams(dimension_semantics=sem, vmem_limit_bytes=VMEM_LIMIT)


def _layer_norm(y, g, b):
    mu = jnp.mean(y, axis=-1, keepdims=True)
    yc = y - mu
    var = jnp.mean(yc * yc, axis=-1, keepdims=True)
    return yc * lax.rsqrt(var + LN_EPS) * g + b


def _modulate(x_ref, mod_ref, base):
    shift = mod_ref[base:base + 1, :]
    scale = mod_ref[base + 1:base + 2, :]
    return (x_ref[...] * (1.0 + scale) + shift).astype(bf16)


def _dot_nt(a, b):
    return lax.dot_general(a, b, (((1,), (1,)), ((), ())), preferred_element_type=f32)


def _ada_kernel(c_ref, w_ref, b_ref, o_ref):
    s = jax.nn.silu(c_ref[...]).astype(bf16)
    o_ref[...] = jnp.dot(s, w_ref[...].astype(bf16), preferred_element_type=f32) + b_ref[...]


def _ada(cond, ada_w, ada_b):
    tn = 2304
    nt = (N_MOD * D) // tn
    out = pl.pallas_call(
        _ada_kernel,
        grid=(DEPTH, nt),
        in_specs=[
            pl.BlockSpec((N_COND, D), lambda l, n: (0, 0)),
            pl.BlockSpec((None, D, tn), lambda l, n: (l, 0, n)),
            pl.BlockSpec((None, 1, tn), lambda l, n: (l, 0, n)),
        ],
        out_specs=pl.BlockSpec((N_COND, tn), lambda l, n: (0, l * nt + n)),
        out_shape=jax.ShapeDtypeStruct((N_COND, DEPTH * N_MOD * D), f32),
        compiler_params=_params("arbitrary", "arbitrary"),
        name="ada_mod",
    )(cond, ada_w, ada_b.reshape(DEPTH, 1, N_MOD * D))
    return out.reshape(N_COND, DEPTH, N_MOD, D)


N_CTX_TILES = T_CTX // FFN_TM
N_TILES = T_ALL // FFN_TM


def _ffn_kernel(*refs, mod_base, tile0, split_x, mixer):
    refs = list(refs)
    o_ref = refs.pop()
    x_refs = [refs.pop(0) for _ in range(2 if split_x else 1)]
    mod_ref = refs.pop(0)
    if mixer:
        zc_ref, zl_ref, wp_ref, gp_ref, bp_ref = [refs.pop(0) for _ in range(5)]
    w1_ref, w2_ref, g_ref, b_ref = refs
    is_ctx = tile0 + pl.program_id(0) < N_CTX_TILES
    shift = mod_ref[mod_base:mod_base + 1, :]
    scale = mod_ref[mod_base + 1:mod_base + 2, :]
    half_gate = 0.5 * mod_ref[mod_base + 2:mod_base + 3, :]
    for r in range(FFN_TM // FFN_SUB):
        rows = slice(r * FFN_SUB, (r + 1) * FFN_SUB)
        x = x_refs[0][rows, :]
        if split_x:
            x = jnp.where(is_ctx, x, x_refs[1][rows, :])
        if mixer:
            z = jnp.where(is_ctx, zc_ref[rows, :], zl_ref[rows, :])
            f = jnp.dot(z, wp_ref[...], preferred_element_type=f32)
            x = _layer_norm(ALPHA * x + mod_ref[5:6, :] * f, gp_ref[...], bp_ref[...])
        h = (x * (1.0 + scale) + shift).astype(bf16)
        gu = jnp.dot(h, w1_ref[...], preferred_element_type=f32)
        a = (jax.nn.silu(gu[:, :D_FF]) * gu[:, D_FF:]).astype(bf16)
        f = jnp.dot(a, w2_ref[...], preferred_element_type=f32)
        o_ref[rows, :] = _layer_norm(ALPHA * x + half_gate * f, g_ref[...], b_ref[...])


def _ffn(xs, mod, w1, w2, ln_g, ln_b, layer, half, mixer=None, tiles=(0, N_TILES), name=None):
    tile0, tile1 = tiles
    resident = pl.Buffered(1)
    ctx_tile = lambda i: (jnp.minimum(tile0 + i, N_CTX_TILES - 1), 0)
    lat_tile = lambda i: (jnp.maximum(tile0 + i - N_CTX_TILES, 0), 0)
    row = lambda k: pl.BlockSpec((None, 1, D), lambda i: (3 * layer + k, 0, 0))
    if len(xs) == 2:
        in_specs = [pl.BlockSpec((FFN_TM, D), ctx_tile), pl.BlockSpec((FFN_TM, D), lat_tile)]
    else:
        in_specs = [pl.BlockSpec((FFN_TM, D), lambda i: (tile0 + i, 0))]
    args = list(xs)
    in_specs.append(pl.BlockSpec((None, None, N_MOD, D), lambda i: (_cond_of_tile(tile0 + i, FFN_TM), layer, 0, 0)))
    args.append(mod)
    if mixer is not None:
        zc, zl, wp, j = mixer
        in_specs += [pl.BlockSpec((FFN_TM, D), ctx_tile), pl.BlockSpec((FFN_TM, D), lat_tile),
                     pl.BlockSpec((None, D, D), lambda i: (j, 0, 0), pipeline_mode=resident), row(1), row(1)]
        args += [zc, zl, wp, ln_g, ln_b]
    in_specs += [
        pl.BlockSpec((None, None, D, 2 * D_FF), lambda i: (layer, half, 0, 0), pipeline_mode=resident),
        pl.BlockSpec((None, None, D_FF, D), lambda i: (layer, half, 0, 0), pipeline_mode=resident),
        row(2 * half), row(2 * half),
    ]
    args += [w1, w2, ln_g, ln_b]
    return pl.pallas_call(
        functools.partial(_ffn_kernel, mod_base=6 * half, tile0=tile0, split_x=len(xs) == 2,
                          mixer=mixer is not None),
        grid=(tile1 - tile0,),
        in_specs=in_specs,
        out_specs=pl.BlockSpec((FFN_TM, D), lambda i: (i, 0)),
        out_shape=jax.ShapeDtypeStruct(((tile1 - tile0) * FFN_TM, D), f32),
        compiler_params=_params("parallel"),
        name=name or f"ffn_l{layer}h{half}",
    )(*args)


def _modmm_kernel(x_ref, mod_ref, w_ref, o_ref):
    o_ref[...] = jnp.dot(_modulate(x_ref, mod_ref, 3), w_ref[...], preferred_element_type=f32).astype(o_ref.dtype)


def _modmm(x, mod, w, layer, j, name):
    n = w.shape[-1]
    tn = 1536
    return pl.pallas_call(
        _modmm_kernel,
        grid=(T_ALL // TM, n // tn),
        in_specs=[
            pl.BlockSpec((TM, D), lambda i, c: (i, 0)),
            pl.BlockSpec((None, None, N_MOD, D), lambda i, c: (_cond_of_tile(i), layer, 0, 0)),
            pl.BlockSpec((None, D, tn), lambda i, c: (j, 0, c)),
        ],
        out_specs=pl.BlockSpec((TM, tn), lambda i, c: (i, c)),
        out_shape=jax.ShapeDtypeStruct((T_ALL, n), bf16),
        compiler_params=_params("parallel", "arbitrary"),
        name=name,
    )(x, mod, w)


def _rope_tables():
    t = np.arange(DEC_SEQ)
    pos = np.stack([t // GRID_W, t % GRID_W], axis=1).astype(np.float64)
    lane = np.arange(LANES)
    hl = lane % HEAD_DIM
    axis = hl // (HEAD_DIM // 2)
    half = HEAD_DIM // 4
    inv = ROPE_BASE ** (-(hl % half).astype(np.float64) / half)
    ang = pos[:, axis] * inv[None, :]
    first = (hl % (HEAD_DIM // 2)) < half
    cos = np.cos(ang)
    sin = np.where(first[None, :], -np.sin(ang), np.sin(ang))
    return (jnp.asarray(cos, f32), jnp.asarray(sin, f32),
            jnp.asarray(cos[:, :HEAD_DIM].T, f32), jnp.asarray(sin[:, :HEAD_DIM].T, f32))


def _rope_rows(x, cos, sin):
    lane = lax.broadcasted_iota(jnp.int32, (x.shape[0], LANES), 1)
    first = (lane % (HEAD_DIM // 2)) < (HEAD_DIM // 4)
    cols = []
    for c in range(x.shape[1] // LANES):
        xc = x[:, c * LANES:(c + 1) * LANES]
        partner = jnp.where(first, pltpu.roll(xc, LANES - HEAD_DIM // 4, 1), pltpu.roll(xc, HEAD_DIM // 4, 1))
        cols.append(xc * cos + partner * sin)
    return jnp.concatenate(cols, axis=1)


def _rope_cols(xT, cosT, sinT):
    q = HEAD_DIM // 4
    heads = []
    for h in range(xT.shape[0] // HEAD_DIM):
        xh = xT[h * HEAD_DIM:(h + 1) * HEAD_DIM, :]
        partner = jnp.concatenate([xh[q:2 * q], xh[0:q], xh[3 * q:4 * q], xh[2 * q:3 * q]], axis=0)
        heads.append(xh * cosT + partner * sinT)
    return jnp.concatenate(heads, axis=0)


def _qkv_ctx_kernel(x_ref, mod_ref, wqT_ref, wkv_ref, wvT_ref, qT_ref, kv_ref, vT_ref):
    h = _modulate(x_ref, mod_ref, 3)
    qT_ref[...] = (_dot_nt(wqT_ref[...], h) * ATTN_SCALE).astype(bf16)
    kv_ref[...] = jnp.dot(h, wkv_ref[...], preferred_element_type=f32)
    vT_ref[...] = _dot_nt(wvT_ref[...], h).astype(bf16)


def _qkv_ctx(x, mod, wqT, wkv, wvT, layer, j):
    return pl.pallas_call(
        _qkv_ctx_kernel,
        grid=(T_CTX // TM,),
        in_specs=[
            pl.BlockSpec((TM, D), lambda i: (i, 0)),
            pl.BlockSpec((None, None, N_MOD, D), lambda i: (0, layer, 0, 0)),
            pl.BlockSpec((None, Q_DIM, D), lambda i: (j, 0, 0)),
            pl.BlockSpec((None, D, 2 * KV_DIM), lambda i: (j, 0, 0)),
            pl.BlockSpec((None, KV_DIM, D), lambda i: (j, 0, 0)),
        ],
        out_specs=[
            pl.BlockSpec((Q_DIM, TM), lambda i: (0, i)),
            pl.BlockSpec((TM, 2 * KV_DIM), lambda i: (i, 0)),
            pl.BlockSpec((KV_DIM, TM), lambda i: (0, i)),
        ],
        out_shape=[
            jax.ShapeDtypeStruct((Q_DIM, T_CTX), bf16),
            jax.ShapeDtypeStruct((T_CTX, 2 * KV_DIM), f32),
            jax.ShapeDtypeStruct((KV_DIM, T_CTX), bf16),
        ],
        compiler_params=_params("parallel"),
        name=f"qkv_ctx_l{layer}",
    )(x, mod, wqT, wkv, wvT)


def _qkv_lat_kernel(x_ref, mod_ref, wqT_ref, wkv_ref, wvT_ref, cos_ref, sin_ref, cosT_ref, sinT_ref,
                    qT_ref, k_ref, vT_ref):
    h = _modulate(x_ref, mod_ref, 3)
    qT = _rope_cols(_dot_nt(wqT_ref[...], h), cosT_ref[...], sinT_ref[...])
    qT_ref[...] = (qT * ATTN_SCALE).astype(bf16)
    k = jnp.dot(h, wkv_ref[:, :KV_DIM], preferred_element_type=f32)
    k_ref[...] = _rope_rows(k, cos_ref[...], sin_ref[...]).astype(bf16)
    vT_ref[...] = _dot_nt(wvT_ref[...], h).astype(bf16)


def _qkv_lat(x, mod, wqT, wkv, wvT, rope, layer, j):
    row0 = T_CTX // DEC_SEQ
    full = lambda shape: pl.BlockSpec(shape, lambda b: (0, 0))
    return pl.pallas_call(
        _qkv_lat_kernel,
        grid=(DEC_BATCH,),
        in_specs=[
            pl.BlockSpec((DEC_SEQ, D), lambda b: (row0 + b, 0)),
            pl.BlockSpec((None, None, N_MOD, D), lambda b: (1 + b, layer, 0, 0)),
            pl.BlockSpec((None, Q_DIM, D), lambda b: (j, 0, 0)),
            pl.BlockSpec((None, D, 2 * KV_DIM), lambda b: (j, 0, 0)),
            pl.BlockSpec((None, KV_DIM, D), lambda b: (j, 0, 0)),
            full((DEC_SEQ, LANES)), full((DEC_SEQ, LANES)), full((HEAD_DIM, DEC_SEQ)), full((HEAD_DIM, DEC_SEQ)),
        ],
        out_specs=[
            pl.BlockSpec((Q_DIM, DEC_SEQ), lambda b: (0, b)),
            pl.BlockSpec((DEC_SEQ, KV_DIM), lambda b: (b, 0)),
            pl.BlockSpec((KV_DIM, DEC_SEQ), lambda b: (0, b)),
        ],
        out_shape=[
            jax.ShapeDtypeStruct((Q_DIM, T_LAT), bf16),
            jax.ShapeDtypeStruct((T_LAT, KV_DIM), bf16),
            jax.ShapeDtypeStruct((KV_DIM, T_LAT), bf16),
        ],
        compiler_params=_params("parallel"),
        name=f"qkv_lat_l{layer}",
    )(x, mod, wqT, wkv, wvT, *rope)


def _softmax_pv_t(parts, sink_row):
    m = sink_row
    for s, _ in parts:
        m = jnp.maximum(m, jnp.max(s, axis=0, keepdims=True))
    denom = jnp.exp(sink_row - m)
    o = None
    for s, vT in parts:
        p = jnp.exp(s - m)
        denom = denom + jnp.sum(p, axis=0, keepdims=True)
        pv = jnp.dot(vT, p.astype(bf16), preferred_element_type=f32)
        o = pv if o is None else o + pv
    return o / denom


def _group_queries(qT_ref, g):
    heads = [qT_ref[(Q_PER_KV * g + i) * HEAD_DIM:(Q_PER_KV * g + i + 1) * HEAD_DIM, :] for i in range(Q_PER_KV)]
    return jnp.concatenate(heads, axis=1)


def _sink_row(sink_ref, g, nq):
    return jnp.concatenate([jnp.full((1, nq), sink_ref[Q_PER_KV * g + i], f32) for i in range(Q_PER_KV)], axis=1)


def _store_heads(o_ref, group_outs, nq):
    heads = [oT[:, i * nq:(i + 1) * nq] for oT in group_outs for i in range(Q_PER_KV)]
    o_ref[...] = jnp.concatenate(heads, axis=0).T.astype(o_ref.dtype)


def _attn_ctx_kernel(sink_ref, qT_ref, kv_ref, vT_ref, o_ref):
    k = kv_ref[:, :KV_DIM].astype(bf16)
    outs = []
    for g in range(KV_HEADS):
        gs = slice(g * HEAD_DIM, (g + 1) * HEAD_DIM)
        s = jnp.dot(k[:, gs], _group_queries(qT_ref, g), preferred_element_type=f32)
        outs.append(_softmax_pv_t([(s, vT_ref[gs, :])], _sink_row(sink_ref, g, SEQ)))
    _store_heads(o_ref, outs, SEQ)


def _attn_ctx(qT, kv, vT, sink):
    return pl.pallas_call(
        _attn_ctx_kernel,
        grid=(BATCH,),
        in_specs=[
            pl.BlockSpec(memory_space=pltpu.SMEM),
            pl.BlockSpec((Q_DIM, SEQ), lambda b: (0, b)),
            pl.BlockSpec((SEQ, 2 * KV_DIM), lambda b: (b, 0)),
            pl.BlockSpec((KV_DIM, SEQ), lambda b: (0, b)),
        ],
        out_specs=pl.BlockSpec((SEQ, D), lambda b: (b, 0)),
        out_shape=jax.ShapeDtypeStruct((T_CTX, D), bf16),
        compiler_params=_params("parallel"),
        name="attn_ctx",
    )(sink, qT, kv, vT)


def _attn_lat_kernel(sink_ref, qT_ref, kp_ref, kc_ref, kn_ref, vp_ref, vc_ref, vn_ref, ck_ref, cv_ref, o_ref,
                     cvT_scr):
    qb = pl.program_id(1)

    @pl.when(qb == 0)
    def _():
        cvT_scr[...] = cv_ref[...].T.astype(bf16)

    n = Q_PER_KV * BLOCK
    row = lax.broadcasted_iota(jnp.int32, (WIN_KEYS, n), 0)
    ql = lax.broadcasted_iota(jnp.int32, (WIN_KEYS, n), 1) & (BLOCK - 1)
    kj = (qb - 1) * BLOCK + row
    rel = row - ql
    valid = (rel >= 0) & (rel <= 2 * WINDOW) & (kj >= 0) & (kj < DEC_SEQ)
    k_win = jnp.concatenate([kp_ref[...], kc_ref[...], kn_ref[...]], axis=0)
    k_ctx = ck_ref[...].astype(bf16)
    outs = []
    for g in range(KV_HEADS):
        gs = slice(g * HEAD_DIM, (g + 1) * HEAD_DIM)
        qg = _group_queries(qT_ref, g)
        s_loc = jnp.where(valid, jnp.dot(k_win[:, gs], qg, preferred_element_type=f32), NEG_INF)
        s_ctx = jnp.dot(k_ctx[:, gs], qg, preferred_element_type=f32)
        v_win = jnp.concatenate([vp_ref[gs, :], vc_ref[gs, :], vn_ref[gs, :]], axis=1)
        outs.append(_softmax_pv_t([(s_loc, v_win), (s_ctx, cvT_scr[gs, :])], _sink_row(sink_ref, g, BLOCK)))
    _store_heads(o_ref, outs, BLOCK)


def _attn_lat(qT, k, vT, cache_k, cache_v, sink, j):
    nqb = DEC_SEQ // BLOCK
    prev = lambda b, i: b * nqb + jnp.maximum(i - 1, 0)
    own = lambda b, i: b * nqb + i
    nxt = lambda b, i: b * nqb + jnp.minimum(i + 1, nqb - 1)
    return pl.pallas_call(
        _attn_lat_kernel,
        grid=(DEC_BATCH, nqb),
        in_specs=[
            pl.BlockSpec(memory_space=pltpu.SMEM),
            pl.BlockSpec((Q_DIM, BLOCK), lambda b, i: (0, own(b, i))),
            pl.BlockSpec((BLOCK, KV_DIM), lambda b, i: (prev(b, i), 0)),
            pl.BlockSpec((BLOCK, KV_DIM), lambda b, i: (own(b, i), 0)),
            pl.BlockSpec((BLOCK, KV_DIM), lambda b, i: (nxt(b, i), 0)),
            pl.BlockSpec((KV_DIM, BLOCK), lambda b, i: (0, prev(b, i))),
            pl.BlockSpec((KV_DIM, BLOCK), lambda b, i: (0, own(b, i))),
            pl.BlockSpec((KV_DIM, BLOCK), lambda b, i: (0, nxt(b, i))),
            pl.BlockSpec((None, None, SEQ, KV_DIM), lambda b, i: (b, j, 0, 0)),
            pl.BlockSpec((None, None, SEQ, KV_DIM), lambda b, i: (b, j, 0, 0)),
        ],
        out_specs=pl.BlockSpec((BLOCK, D), lambda b, i: (own(b, i), 0)),
        out_shape=jax.ShapeDtypeStruct((T_LAT, D), bf16),
        scratch_shapes=[pltpu.VMEM((KV_DIM, SEQ), bf16)],
        compiler_params=_params("parallel", "arbitrary"),
        name="attn_lat",
    )(sink, qT, k, k, k, vT, vT, vT, cache_k, cache_v)


def _dft_tables(L):
    idx = np.arange(L)
    ft = np.outer(idx, idx) % (2 * L)
    cm = np.cos(np.pi * ft / L)
    sm = np.sin(np.pi * ft / L)
    nyq = np.where(idx % 2 == 0, 1.0, -1.0)
    sm_n = sm.copy()
    sm_n[0, :] = nyq
    fwd = np.concatenate([cm, sm_n], axis=0)
    wc = np.full((L,), 2.0)
    wc[0] = 1.0
    inv_c = cm.T * wc[None, :]
    inv_s = sm.T * 2.0
    inv_s[:, 0] = nyq
    inv = np.concatenate([inv_c, inv_s], axis=1) / (2 * L)
    return cm, sm, nyq, fwd, inv


def _filter_tables(L):
    t = np.arange(L, dtype=np.float64) / L
    bands = np.arange(1, FILTER_BANDS + 1, dtype=np.float64)
    ph = 2 * np.pi * t[:, None] * bands[None]
    feats = np.zeros((L, LANES))
    feats[:, :FILTER_EMB] = np.concatenate([t[:, None], np.sin(ph), np.cos(ph)], -1)
    max_decay = math.log(DECAY_TARGET) / DECAY_FAST_PCT
    min_decay = math.log(DECAY_TARGET) / DECAY_SLOW_PCT
    deltas = np.abs(np.linspace(min_decay, max_decay, D))
    decay = np.exp(-t[:, None] * deltas[None])
    return feats, decay


def _dot_hi(a, b):
    return jnp.dot(a, b, preferred_element_type=f32, precision=lax.Precision.HIGHEST)


def _split_bf16(a):
    hi = a.astype(bf16)
    return hi, (a - hi.astype(f32)).astype(bf16)


def _dot_split(t_hi, t_lo, b):
    b_hi, b_lo = _split_bf16(b)
    return (jnp.dot(t_hi, b_hi, preferred_element_type=f32) + jnp.dot(t_hi, b_lo, preferred_element_type=f32)
            + jnp.dot(t_lo, b_hi, preferred_element_type=f32))


def _filter_kernel(feat_ref, w1_ref, b1_ref, w2_ref, b2_ref, w3f_ref, w3b_ref, dec_ref, cmh_ref, cml_ref, smh_ref,
                   sml_ref, nyq_ref, o_ref, a_scr):
    @pl.when((pl.program_id(0) == 0) & (pl.program_id(1) == 0))
    def _():
        a1 = jnp.sin(_dot_hi(feat_ref[...], w1_ref[...]) + b1_ref[...])
        a_scr[...] = jnp.sin(_dot_hi(a1, w2_ref[...]) + b2_ref[...])

    a = a_scr[...]
    dec = dec_ref[...]
    fwd = _dot_hi(a, w3f_ref[...]) * dec
    bwd = _dot_hi(a, w3b_ref[...]) * dec
    row = lax.broadcasted_iota(jnp.int32, fwd.shape, 0)
    bwd = jnp.where(row == 0, 0.0, bwd)
    norm = jnp.sum(jnp.abs(fwd), axis=0, keepdims=True) + jnp.sum(jnp.abs(bwd), axis=0, keepdims=True) + 1e-6
    even = (fwd + bwd) / norm
    odd = (bwd - fwd) / norm
    hr = _dot_split(cmh_ref[...], cml_ref[...], even)
    hi = _dot_split(smh_ref[...], sml_ref[...], odd)
    h_nyq = jnp.sum(even * nyq_ref[...], axis=0, keepdims=True)
    o_ref[0] = hr
    o_ref[1] = jnp.where(row == 0, 0.0, hi)
    o_ref[2] = jnp.where(row == 0, h_nyq, hr)


def _hyena_filter(L, j, w1p, b1p, w2p, b2p, w3p, name):
    cm, sm, nyq, _, _ = _dft_tables(L)
    feats, decay = _filter_tables(L)
    td = 512
    nd = D // td
    full = lambda shape: pl.BlockSpec(shape, lambda o, d: (0,) * len(shape))
    return pl.pallas_call(
        _filter_kernel,
        grid=(2, nd),
        in_specs=[
            full((L, LANES)),
            pl.BlockSpec((None, LANES, LANES), lambda o, d: (j, 0, 0)),
            pl.BlockSpec((None, 1, LANES), lambda o, d: (j, 0, 0)),
            pl.BlockSpec((None, LANES, LANES), lambda o, d: (j, 0, 0)),
            pl.BlockSpec((None, 1, LANES), lambda o, d: (j, 0, 0)),
            pl.BlockSpec((None, LANES, td), lambda o, d: (j, 0, (2 * o) * nd + d)),
            pl.BlockSpec((None, LANES, td), lambda o, d: (j, 0, (2 * o + 1) * nd + d)),
            pl.BlockSpec((L, td), lambda o, d: (0, d)),
            full((L, L)), full((L, L)), full((L, L)), full((L, L)),
            full((L, 1)),
        ],
        out_specs=pl.BlockSpec((3, L, td), lambda o, d: (o, 0, d)),
        out_shape=jax.ShapeDtypeStruct((6, L, D), f32),
        scratch_shapes=[pltpu.VMEM((L, LANES), f32)],
        compiler_params=_params("arbitrary", "arbitrary"),
        name=name,
    )(jnp.asarray(feats, f32), w1p, b1p, w2p, b2p, w3p, w3p, jnp.asarray(decay, f32),
      *_np_split_bf16(cm), *_np_split_bf16(sm), jnp.asarray(nyq[:, None], f32))


def _np_split_bf16(t):
    hi = t.astype(bf16)
    lo = (t - hi.astype(np.float64)).astype(bf16)
    return jnp.asarray(hi), jnp.asarray(lo)


def _short_conv(u, w, b):
    L = u.shape[0]
    row = lax.broadcasted_iota(jnp.int32, u.shape, 0)
    prev = jnp.where(row == 0, 0.0, pltpu.roll(u, 1, 0))
    nxt = jnp.where(row == L - 1, 0.0, pltpu.roll(u, L - 1, 0))
    return prev * w[0:1, :] + u * w[1:2, :] + nxt * w[2:3, :] + b


def _spectral_product(zs, h_ref, o):
    L = zs.shape[0] // 2
    zr, zi = zs[:L], zs[L:]
    hr, hi_m, hr_n = h_ref[3 * o], h_ref[3 * o + 1], h_ref[3 * o + 2]
    return jnp.concatenate([zr * hr + zi * hi_m, zi * hr_n - zr * hi_m], axis=0).astype(bf16)


def _hyena_conv_kernel(pv_ref, p1_ref, p2_ref, wv_ref, w1_ref, w2_ref, bv_ref, b1_ref, b2_ref, h_ref, d_ref,
                       fwd_ref, inv_ref, o_ref):
    L = h_ref.shape[1]
    seqs = [slice(s * L, (s + 1) * L) for s in range(pv_ref.shape[0] // L)]
    dft = lambda z: jnp.dot(fwd_ref[...], z.astype(bf16), preferred_element_type=f32)
    idft = lambda y: jnp.dot(inv_ref[...], y, preferred_element_type=f32)
    z = [_short_conv(pv_ref[r, :].astype(f32), wv_ref[...], bv_ref[...]) for r in seqs]
    gate_in = [(p1_ref, w1_ref, b1_ref), (p2_ref, w2_ref, b2_ref)]
    for o in range(2):
        zs = [dft(zz) for zz in z]
        ys = [_spectral_product(s, h_ref, o) for s in zs]
        y = [idft(s) for s in ys]
        p_ref, w_ref, b_ref = gate_in[o]
        gates = [_short_conv(p_ref[r, :].astype(f32), w_ref[...], b_ref[...]) for r in seqs]
        z = [g * (yy + zz * d_ref[o:o + 1, :]) for g, yy, zz in zip(gates, y, z)]
    for r, zz in zip(seqs, z):
        o_ref[r, :] = zz.astype(bf16)


def _hyena_conv(p, conv_w, conv_b, spec, hy_d, j, L, nb, nseq, row0, td, name):
    _, _, _, fwd, inv = _dft_tables(L)
    nd = D // td
    once = pl.Buffered(1)
    in_specs = []
    for c in range(3):
        in_specs.append(pl.BlockSpec((nseq * L, td), lambda d, b, c=c: (row0 // nseq + b, c * nd + d)))
    for c in range(3):
        in_specs.append(pl.BlockSpec((None, 3, td), lambda d, b, c=c: (j, 0, c * nd + d)))
    for c in range(3):
        in_specs.append(pl.BlockSpec((None, 1, td), lambda d, b, c=c: (j, 0, c * nd + d)))
    in_specs += [
        pl.BlockSpec((6, L, td), lambda d, b: (0, 0, d), pipeline_mode=once),
        pl.BlockSpec((None, 2, td), lambda d, b: (j, 0, d)),
        pl.BlockSpec((2 * L, L), lambda d, b: (0, 0), pipeline_mode=once),
        pl.BlockSpec((L, 2 * L), lambda d, b: (0, 0), pipeline_mode=once),
    ]
    return pl.pallas_call(
        _hyena_conv_kernel,
        grid=(nd, nb // nseq),
        in_specs=in_specs,
        out_specs=pl.BlockSpec((nseq * L, td), lambda d, b: (b, d)),
        out_shape=jax.ShapeDtypeStruct((nb * L, D), bf16),
        compiler_params=_params("parallel", "parallel"),
        name=name,
    )(p, p, p, conv_w, conv_w, conv_w, conv_b, conv_b, conv_b, spec, hy_d,
      jnp.asarray(fwd, f32).astype(bf16), jnp.asarray(inv, f32).astype(bf16))


def _pad2(a, rows, cols):
    return jnp.pad(a, [(0, 0)] * (a.ndim - 2) + [(0, rows - a.shape[-2]), (0, cols - a.shape[-1])])


def kernel(x_prompt, x_sample, cache_k, cache_v, c, c_ctx, ada_w, ada_b, ln_g, ln_b, ffn_w1, ffn_w2, attn_w_qkv,
           attn_w_o, attn_sink, hy_w_in, hy_conv_w, hy_conv_b, hy_f_w1, hy_f_b1, hy_f_w2, hy_f_b2, hy_f_w3, hy_d,
           hy_w_out):
    cond =jnp.concatenate([c_ctx[None], c, jnp.zeros((N_COND - 1 - DEC_BATCH, D), f32)], axis=0)
    mod = _ada(cond, ada_w, ada_b)

    w1 = ffn_w1.astype(bf16)
    w2 = ffn_w2.astype(bf16)
    w_qkv = attn_w_qkv.astype(bf16)
    wqT = jnp.swapaxes(w_qkv[:, :, :Q_DIM], 1, 2)
    wkv = w_qkv[:, :, Q_DIM:]
    wvT = jnp.swapaxes(w_qkv[:, :, Q_DIM + KV_DIM:], 1, 2)
    w_o = attn_w_o.astype(bf16)
    w_in = hy_w_in.astype(bf16)
    w_out = hy_w_out.astype(bf16)
    ck = cache_k.reshape(DEC_BATCH, -1, SEQ, KV_DIM)
    cv = cache_v.reshape(DEC_BATCH, -1, SEQ, KV_DIM)
    rope = _rope_tables()
    fw1 = _pad2(hy_f_w1, LANES, LANES)
    fb1 = _pad2(hy_f_b1[:, None, :], 1, LANES)
    fw2 = _pad2(hy_f_w2, LANES, LANES)
    fb2 = _pad2(hy_f_b2[:, None, :], 1, LANES)
    fw3 = _pad2(hy_f_w3, LANES, hy_f_w3.shape[-1])
    conv_b = hy_conv_b[:, None, :]
    ln_g = ln_g.reshape(DEPTH * 3, 1, D)
    ln_b = ln_b.reshape(DEPTH * 3, 1, D)

    new_k, new_v = [], []
    xs = (x_prompt.reshape(T_CTX, D), x_sample.reshape(T_LAT, D))
    for i in range(DEPTH):
        j = i // 2
        x = _ffn(xs, mod, w1, w2, ln_g, ln_b, i, 0)
        if i % 2 == 0:
            qT_c, kv_c, vT_c = _qkv_ctx(x, mod, wqT, wkv, wvT, i, j)
            qT_l, k_l, vT_l = _qkv_lat(x, mod, wqT, wkv, wvT, rope, i, j)
            new_k.append(kv_c[:, :KV_DIM].reshape(BATCH, SEQ, KV_HEADS, HEAD_DIM))
            new_v.append(kv_c[:, KV_DIM:].reshape(BATCH, SEQ, KV_HEADS, HEAD_DIM))
            sink = attn_sink[j]
            mixer = (_attn_ctx(qT_c, kv_c, vT_c, sink), _attn_lat(qT_l, k_l, vT_l, ck, cv, sink, j), w_o, j)
        else:
            p = _modmm(x, mod, w_in, i, j, f"hy_in_l{i}")
            spec_ctx = _hyena_filter(SEQ, j, fw1, fb1, fw2, fb2, fw3, f"hy_filter_ctx_l{i}")
            spec_lat = _hyena_filter(DEC_SEQ, j, fw1, fb1, fw2, fb2, fw3, f"hy_filter_lat_l{i}")
            z_c = _hyena_conv(p, hy_conv_w, conv_b, spec_ctx, hy_d, j, SEQ, BATCH, 4, 0, 512, f"hy_conv_ctx_l{i}")
            z_l = _hyena_conv(p, hy_conv_w, conv_b, spec_lat, hy_d, j, DEC_SEQ, DEC_BATCH, 2, T_CTX // DEC_SEQ, 256,
                              f"hy_conv_lat_l{i}")
            mixer = (z_c, z_l, w_out, j)
        if i < DEPTH - 1:
            xs = (_ffn((x,), mod, w1, w2, ln_g, ln_b, i, 1, mixer=mixer),)
    y_prompt = _ffn((x,), mod, w1, w2, ln_g, ln_b, DEPTH - 1, 1, mixer=mixer, tiles=(0, N_CTX_TILES),
                    name="ffn_last_ctx")
    y_sample = _ffn((x,), mod, w1, w2, ln_g, ln_b, DEPTH - 1, 1, mixer=mixer, tiles=(N_CTX_TILES, N_TILES),
                    name="ffn_last_lat")
    return (y_prompt.reshape(BATCH, SEQ, D), y_sample.reshape(DEC_BATCH, DEC_SEQ, D),
            jnp.stack(new_k, axis=1), jnp.stack(new_v, axis=1))
```

```python
import functools
import math

import numpy as np
import jax
import jax.numpy as jnp
from jax import lax
from jax.experimental import pallas as pl
from jax.experimental.pallas import tpu as pltpu

D = 1024
BATCH, SEQ = 16, 256
DEC_BATCH, DEC_SEQ = 8, 1024
DEPTH = 4
N_HEADS, HEAD_DIM, KV_HEADS = 16, 64, 4
Q_PER_KV = N_HEADS // KV_HEADS
Q_DIM = N_HEADS * HEAD_DIM
KV_DIM = KV_HEADS * HEAD_DIM
GRID_W = 64
WINDOW = 128
BLOCK = 128
WIN_KEYS = BLOCK + 2 * WINDOW
ROPE_BASE = 10000.0
ATTN_SCALE = HEAD_DIM ** -0.5
D_FF = 2816
N_MOD = 9
LN_EPS = 1e-5
ALPHA = (2 * DEPTH) ** 0.25
NEG_INF = -1e30
FILTER_BANDS = 16
FILTER_EMB = 1 + 2 * FILTER_BANDS
DECAY_FAST_PCT, DECAY_SLOW_PCT, DECAY_TARGET = 0.3, 1.5, 1e-2

T_CTX = BATCH * SEQ
T_LAT = DEC_BATCH * DEC_SEQ
T_ALL = T_CTX + T_LAT
N_COND = 16
TM = 1024
FFN_TM = 512
FFN_SUB = 256
LANES = 128
VMEM_LIMIT = 56 * 1024 * 1024

f32 = jnp.float32
bf16 = jnp.bfloat16


def _cond_of_tile(i, tm=TM):
    return jnp.where(i < T_CTX // tm, 0, 1 + (i * tm - T_CTX) // DEC_SEQ)


def _params(*sem):
    return pltpu.CompilerParams(dimension_semantics=sem, vmem_limit_bytes=VMEM_LIMIT)


def _layer_norm(y, g, b):
    mu = jnp.mean(y, axis=-1, keepdims=True)
    yc = y - mu
    var = jnp.mean(yc * yc, axis=-1, keepdims=True)
    return yc * lax.rsqrt(var + LN_EPS) * g + b


def _modulate(x_ref, mod_ref, base):
    shift = mod_ref[base:base + 1, :]
    scale = mod_ref[base + 1:base + 2, :]
    return (x_ref[...] * (1.0 + scale) + shift).astype(bf16)


def _dot_nt(a, b):
    return lax.dot_general(a, b, (((1,), (1,)), ((), ())), preferred_element_type=f32)


def _ada_kernel(c_ref, w_ref, b_ref, o_ref):
    s = jax.nn.silu(c_ref[...]).astype(bf16)
    o_ref[...] = jnp.dot(s, w_ref[...].astype(bf16), preferred_element_type=f32) + b_ref[...]


def _ada(cond, ada_w, ada_b):
    tn = 2304
    nt = (N_MOD * D) // tn
    out = pl.pallas_call(
        _ada_kernel,
        grid=(DEPTH, nt),
        in_specs=[
            pl.BlockSpec((N_COND, D), lambda l, n: (0, 0)),
            pl.BlockSpec((None, D, tn), lambda l, n: (l, 0, n)),
            pl.BlockSpec((None, 1, tn), lambda l, n: (l, 0, n)),
        ],
        out_specs=pl.BlockSpec((N_COND, tn), lambda l, n: (0, l * nt + n)),
        out_shape=jax.ShapeDtypeStruct((N_COND, DEPTH * N_MOD * D), f32),
        compiler_params=_params("arbitrary", "arbitrary"),
        name="ada_mod",
    )(cond, ada_w, ada_b.reshape(DEPTH, 1, N_MOD * D))
    return out.reshape(N_COND, DEPTH, N_MOD, D)


N_CTX_TILES = T_CTX // FFN_TM
N_TILES = T_ALL // FFN_TM


def _ffn_kernel(*refs, mod_base, tile0, split_x, mixer):
    refs = list(refs)
    o_ref = refs.pop()
    x_refs = [refs.pop(0) for _ in range(2 if split_x else 1)]
    mod_ref = refs.pop(0)
    if mixer:
        zc_ref, zl_ref, wp_ref, gp_ref, bp_ref = [refs.pop(0) for _ in range(5)]
    w1_ref, w2_ref, g_ref, b_ref = refs
    is_ctx = tile0 + pl.program_id(0) < N_CTX_TILES
    shift = mod_ref[mod_base:mod_base + 1, :]
    scale = mod_ref[mod_base + 1:mod_base + 2, :]
    half_gate = 0.5 * mod_ref[mod_base + 2:mod_base + 3, :]
    for r in range(FFN_TM // FFN_SUB):
        rows = slice(r * FFN_SUB, (r + 1) * FFN_SUB)
        x = x_refs[0][rows, :]
        if split_x:
            x = jnp.where(is_ctx, x, x_refs[1][rows, :])
        if mixer:
            z = jnp.where(is_ctx, zc_ref[rows, :], zl_ref[rows, :])
            f = jnp.dot(z, wp_ref[...], preferred_element_type=f32)
            x = _layer_norm(ALPHA * x + mod_ref[5:6, :] * f, gp_ref[...], bp_ref[...])
        h = (x * (1.0 + scale) + shift).astype(bf16)
        gu = jnp.dot(h, w1_ref[...], preferred_element_type=f32)
        a = (jax.nn.silu(gu[:, :D_FF]) * gu[:, D_FF:]).astype(bf16)
        f = jnp.dot(a, w2_ref[...], preferred_element_type=f32)
        o_ref[rows, :] = _layer_norm(ALPHA * x + half_gate * f, g_ref[...], b_ref[...])


def _ffn(xs, mod, w1, w2, ln_g, ln_b, layer, half, mixer=None, tiles=(0, N_TILES), name=None):
    tile0, tile1 = tiles
    resident = pl.Buffered(1)
    ctx_tile = lambda i: (jnp.minimum(tile0 + i, N_CTX_TILES - 1), 0)
    lat_tile = lambda i: (jnp.maximum(tile0 + i - N_CTX_TILES, 0), 0)
    row = lambda k: pl.BlockSpec((None, 1, D), lambda i: (3 * layer + k, 0, 0))
    if len(xs) == 2:
        in_specs = [pl.BlockSpec((FFN_TM, D), ctx_tile), pl.BlockSpec((FFN_TM, D), lat_tile)]
    else:
        in_specs = [pl.BlockSpec((FFN_TM, D), lambda i: (tile0 + i, 0))]
    args = list(xs)
    in_specs.append(pl.BlockSpec((None, None, N_MOD, D), lambda i: (_cond_of_tile(tile0 + i, FFN_TM), layer, 0, 0)))
    args.append(mod)
    if mixer is not None:
        zc, zl, wp, j = mixer
        in_specs += [pl.BlockSpec((FFN_TM, D), ctx_tile), pl.BlockSpec((FFN_TM, D), lat_tile),
                     pl.BlockSpec((None, D, D), lambda i: (j, 0, 0), pipeline_mode=resident), row(1), row(1)]
        args += [zc, zl, wp, ln_g, ln_b]
    in_specs += [
        pl.BlockSpec((None, None, D, 2 * D_FF), lambda i: (layer, half, 0, 0), pipeline_mode=resident),
        pl.BlockSpec((None, None, D_FF, D), lambda i: (layer, half, 0, 0), pipeline_mode=resident),
        row(2 * half), row(2 * half),
    ]
    args += [w1, w2, ln_g, ln_b]
    return pl.pallas_call(
        functools.partial(_ffn_kernel, mod_base=6 * half, tile0=tile0, split_x=len(xs) == 2,
                          mixer=mixer is not None),
        grid=(tile1 - tile0,),
        in_specs=in_specs,
        out_specs=pl.BlockSpec((FFN_TM, D), lambda i: (i, 0)),
        out_shape=jax.ShapeDtypeStruct(((tile1 - tile0) * FFN_TM, D), f32),
        compiler_params=_params("parallel"),
        name=name or f"ffn_l{layer}h{half}",
    )(*args)


def _modmm_kernel(x_ref, mod_ref, w_ref, o_ref):
    o_ref[...] = jnp.dot(_modulate(x_ref, mod_ref, 3), w_ref[...], preferred_element_type=f32).astype(o_ref.dtype)


def _modmm(x, mod, w, layer, j, name):
    n = w.shape[-1]
    tn = 1536
    return pl.pallas_call(
        _modmm_kernel,
        grid=(T_ALL // TM, n // tn),
        in_specs=[
            pl.BlockSpec((TM, D), lambda i, c: (i, 0)),
            pl.BlockSpec((None, None, N_MOD, D), lambda i, c: (_cond_of_tile(i), layer, 0, 0)),
            pl.BlockSpec((None, D, tn), lambda i, c: (j, 0, c)),
        ],
        out_specs=pl.BlockSpec((TM, tn), lambda i, c: (i, c)),
        out_shape=jax.ShapeDtypeStruct((T_ALL, n), bf16),
        compiler_params=_params("parallel", "arbitrary"),
        name=name,
    )(x, mod, w)


def _rope_tables():
    t = np.arange(DEC_SEQ)
    pos = np.stack([t // GRID_W, t % GRID_W], axis=1).astype(np.float64)
    lane = np.arange(LANES)
    hl = lane % HEAD_DIM
    axis = hl // (HEAD_DIM // 2)
    half = HEAD_DIM // 4
    inv = ROPE_BASE ** (-(hl % half).astype(np.float64) / half)
    ang = pos[:, axis] * inv[None, :]
    first = (hl % (HEAD_DIM // 2)) < half
    cos = np.cos(ang)
    sin = np.where(first[None, :], -np.sin(ang), np.sin(ang))
    return (jnp.asarray(cos, f32), jnp.asarray(sin, f32),
            jnp.asarray(cos[:, :HEAD_DIM].T, f32), jnp.asarray(sin[:, :HEAD_DIM].T, f32))


def _rope_rows(x, cos, sin):
    lane = lax.broadcasted_iota(jnp.int32, (x.shape[0], LANES), 1)
    first = (lane % (HEAD_DIM // 2)) < (HEAD_DIM // 4)
    cols = []
    for c in range(x.shape[1] // LANES):
        xc = x[:, c * LANES:(c + 1) * LANES]
        partner = jnp.where(first, pltpu.roll(xc, LANES - HEAD_DIM // 4, 1), pltpu.roll(xc, HEAD_DIM // 4, 1))
        cols.append(xc * cos + partner * sin)
    return jnp.concatenate(cols, axis=1)


def _rope_cols(xT, cosT, sinT):
    q = HEAD_DIM // 4
    heads = []
    for h in range(xT.shape[0] // HEAD_DIM):
        xh = xT[h * HEAD_DIM:(h + 1) * HEAD_DIM, :]
        partner = jnp.concatenate([xh[q:2 * q], xh[0:q], xh[3 * q:4 * q], xh[2 * q:3 * q]], axis=0)
        heads.append(xh * cosT + partner * sinT)
    return jnp.concatenate(heads, axis=0)


def _qkv_ctx_kernel(x_ref, mod_ref, wqT_ref, wkv_ref, wvT_ref, qT_ref, kv_ref, vT_ref):
    h = _modulate(x_ref, mod_ref, 3)
    qT_ref[...] = (_dot_nt(wqT_ref[...], h) * ATTN_SCALE).astype(bf16)
    kv_ref[...] = jnp.dot(h, wkv_ref[...], preferred_element_type=f32)
    vT_ref[...] = _dot_nt(wvT_ref[...], h).astype(bf16)


def _qkv_ctx(x, mod, wqT, wkv, wvT, layer, j):
    return pl.pallas_call(
        _qkv_ctx_kernel,
        grid=(T_CTX // TM,),
        in_specs=[
            pl.BlockSpec((TM, D), lambda i: (i, 0)),
            pl.BlockSpec((None, None, N_MOD, D), lambda i: (0, layer, 0, 0)),
            pl.BlockSpec((None, Q_DIM, D), lambda i: (j, 0, 0)),
            pl.BlockSpec((None, D, 2 * KV_DIM), lambda i: (j, 0, 0)),
            pl.BlockSpec((None, KV_DIM, D), lambda i: (j, 0, 0)),
        ],
        out_specs=[
            pl.BlockSpec((Q_DIM, TM), lambda i: (0, i)),
            pl.BlockSpec((TM, 2 * KV_DIM), lambda i: (i, 0)),
            pl.BlockSpec((KV_DIM, TM), lambda i: (0, i)),
        ],
        out_shape=[
            jax.ShapeDtypeStruct((Q_DIM, T_CTX), bf16),
            jax.ShapeDtypeStruct((T_CTX, 2 * KV_DIM), f32),
            jax.ShapeDtypeStruct((KV_DIM, T_CTX), bf16),
        ],
        compiler_params=_params("parallel"),
        name=f"qkv_ctx_l{layer}",
    )(x, mod, wqT, wkv, wvT)


def _qkv_lat_kernel(x_ref, mod_ref, wqT_ref, wkv_ref, wvT_ref, cos_ref, sin_ref, cosT_ref, sinT_ref,
                    qT_ref, k_ref, vT_ref):
    h = _modulate(x_ref, mod_ref, 3)
    qT = _rope_cols(_dot_nt(wqT_ref[...], h), cosT_ref[...], sinT_ref[...])
    qT_ref[...] = (qT * ATTN_SCALE).astype(bf16)
    k = jnp.dot(h, wkv_ref[:, :KV_DIM], preferred_element_type=f32)
    k_ref[...] = _rope_rows(k, cos_ref[...], sin_ref[...]).astype(bf16)
    vT_ref[...] = _dot_nt(wvT_ref[...], h).astype(bf16)


def _qkv_lat(x, mod, wqT, wkv, wvT, rope, layer, j):
    row0 = T_CTX // DEC_SEQ
    full = lambda shape: pl.BlockSpec(shape, lambda b: (0, 0))
    return pl.pallas_call(
        _qkv_lat_kernel,
        grid=(DEC_BATCH,),
        in_specs=[
            pl.BlockSpec((DEC_SEQ, D), lambda b: (row0 + b, 0)),
            pl.BlockSpec((None, None, N_MOD, D), lambda b: (1 + b, layer, 0, 0)),
            pl.BlockSpec((None, Q_DIM, D), lambda b: (j, 0, 0)),
            pl.BlockSpec((None, D, 2 * KV_DIM), lambda b: (j, 0, 0)),
            pl.BlockSpec((None, KV_DIM, D), lambda b: (j, 0, 0)),
            full((DEC_SEQ, LANES)), full((DEC_SEQ, LANES)), full((HEAD_DIM, DEC_SEQ)), full((HEAD_DIM, DEC_SEQ)),
        ],
        out_specs=[
            pl.BlockSpec((Q_DIM, DEC_SEQ), lambda b: (0, b)),
            pl.BlockSpec((DEC_SEQ, KV_DIM), lambda b: (b, 0)),
            pl.BlockSpec((KV_DIM, DEC_SEQ), lambda b: (0, b)),
        ],
        out_shape=[
            jax.ShapeDtypeStruct((Q_DIM, T_LAT), bf16),
            jax.ShapeDtypeStruct((T_LAT, KV_DIM), bf16),
            jax.ShapeDtypeStruct((KV_DIM, T_LAT), bf16),
        ],
        compiler_params=_params("parallel"),
        name=f"qkv_lat_l{layer}",
    )(x, mod, wqT, wkv, wvT, *rope)


def _softmax_pv_t(parts, sink_row):
    m = sink_row
    for s, _ in parts:
        m = jnp.maximum(m, jnp.max(s, axis=0, keepdims=True))
    denom = jnp.exp(sink_row - m)
    o = None
    for s, vT in parts:
        p = jnp.exp(s - m)
        denom = denom + jnp.sum(p, axis=0, keepdims=True)
        pv = jnp.dot(vT, p.astype(bf16), preferred_element_type=f32)
        o = pv if o is None else o + pv
    return o / denom


def _group_queries(qT_ref, g):
    heads = [qT_ref[(Q_PER_KV * g + i) * HEAD_DIM:(Q_PER_KV * g + i + 1) * HEAD_DIM, :] for i in range(Q_PER_KV)]
    return jnp.concatenate(heads, axis=1)


def _sink_row(sink_ref, g, nq):
    return jnp.concatenate([jnp.full((1, nq), sink_ref[Q_PER_KV * g + i], f32) for i in range(Q_PER_KV)], axis=1)


def _store_heads(o_ref, group_outs, nq):
    heads = [oT[:, i * nq:(i + 1) * nq] for oT in group_outs for i in range(Q_PER_KV)]
    o_ref[...] = jnp.concatenate(heads, axis=0).T.astype(o_ref.dtype)


def _attn_ctx_kernel(sink_ref, qT_ref, kv_ref, vT_ref, o_ref):
    k = kv_ref[:, :KV_DIM].astype(bf16)
    outs = []
    for g in range(KV_HEADS):
        gs = slice(g * HEAD_DIM, (g + 1) * HEAD_DIM)
        s = jnp.dot(k[:, gs], _group_queries(qT_ref, g), preferred_element_type=f32)
        outs.append(_softmax_pv_t([(s, vT_ref[gs, :])], _sink_row(sink_ref, g, SEQ)))
    _store_heads(o_ref, outs, SEQ)


def _attn_ctx(qT, kv, vT, sink):
    return pl.pallas_call(
        _attn_ctx_kernel,
        grid=(BATCH,),
        in_specs=[
            pl.BlockSpec(memory_space=pltpu.SMEM),
            pl.BlockSpec((Q_DIM, SEQ), lambda b: (0, b)),
            pl.BlockSpec((SEQ, 2 * KV_DIM), lambda b: (b, 0)),
            pl.BlockSpec((KV_DIM, SEQ), lambda b: (0, b)),
        ],
        out_specs=pl.BlockSpec((SEQ, D), lambda b: (b, 0)),
        out_shape=jax.ShapeDtypeStruct((T_CTX, D), bf16),
        compiler_params=_params("parallel"),
        name="attn_ctx",
    )(sink, qT, kv, vT)


def _attn_lat_kernel(sink_ref, qT_ref, kp_ref, kc_ref, kn_ref, vp_ref, vc_ref, vn_ref, ck_ref, cv_ref, o_ref,
                     cvT_scr):
    qb = pl.program_id(1)

    @pl.when(qb == 0)
    def _():
        cvT_scr[...] = cv_ref[...].T.astype(bf16)

    n = Q_PER_KV * BLOCK
    row = lax.broadcasted_iota(jnp.int32, (WIN_KEYS, n), 0)
    ql = lax.broadcasted_iota(jnp.int32, (WIN_KEYS, n), 1) & (BLOCK - 1)
    kj = (qb - 1) * BLOCK + row
    rel = row - ql
    valid = (rel >= 0) & (rel <= 2 * WINDOW) & (kj >= 0) & (kj < DEC_SEQ)
    k_win = jnp.concatenate([kp_ref[...], kc_ref[...], kn_ref[...]], axis=0)
    k_ctx = ck_ref[...].astype(bf16)
    outs = []
    for g in range(KV_HEADS):
        gs = slice(g * HEAD_DIM, (g + 1) * HEAD_DIM)
        qg = _group_queries(qT_ref, g)
        s_loc = jnp.where(valid, jnp.dot(k_win[:, gs], qg, preferred_element_type=f32), NEG_INF)
        s_ctx = jnp.dot(k_ctx[:, gs], qg, preferred_element_type=f32)
        v_win = jnp.concatenate([vp_ref[gs, :], vc_ref[gs, :], vn_ref[gs, :]], axis=1)
        outs.append(_softmax_pv_t([(s_loc, v_win), (s_ctx, cvT_scr[gs, :])], _sink_row(sink_ref, g, BLOCK)))
    _store_heads(o_ref, outs, BLOCK)


def _attn_lat(qT, k, vT, cache_k, cache_v, sink, j):
    nqb = DEC_SEQ // BLOCK
    prev = lambda b, i: b * nqb + jnp.maximum(i - 1, 0)
    own = lambda b, i: b * nqb + i
    nxt = lambda b, i: b * nqb + jnp.minimum(i + 1, nqb - 1)
    return pl.pallas_call(
        _attn_lat_kernel,
        grid=(DEC_BATCH, nqb),
        in_specs=[
            pl.BlockSpec(memory_space=pltpu.SMEM),
            pl.BlockSpec((Q_DIM, BLOCK), lambda b, i: (0, own(b, i))),
            pl.BlockSpec((BLOCK, KV_DIM), lambda b, i: (prev(b, i), 0)),
            pl.BlockSpec((BLOCK, KV_DIM), lambda b, i: (own(b, i), 0)),
            pl.BlockSpec((BLOCK, KV_DIM), lambda b, i: (nxt(b, i), 0)),
            pl.BlockSpec((KV_DIM, BLOCK), lambda b, i: (0, prev(b, i))),
            pl.BlockSpec((KV_DIM, BLOCK), lambda b, i: (0, own(b, i))),
            pl.BlockSpec((KV_DIM, BLOCK), lambda b, i: (0, nxt(b, i))),
            pl.BlockSpec((None, None, SEQ, KV_DIM), lambda b, i: (b, j, 0, 0)),
            pl.BlockSpec((None, None, SEQ, KV_DIM), lambda b, i: (b, j, 0, 0)),
        ],
        out_specs=pl.BlockSpec((BLOCK, D), lambda b, i: (own(b, i), 0)),
        out_shape=jax.ShapeDtypeStruct((T_LAT, D), bf16),
        scratch_shapes=[pltpu.VMEM((KV_DIM, SEQ), bf16)],
        compiler_params=_params("parallel", "arbitrary"),
        name="attn_lat",
    )(sink, qT, k, k, k, vT, vT, vT, cache_k, cache_v)


def _dft_tables(L):
    idx = np.arange(L)
    ft = np.outer(idx, idx) % (2 * L)
    cm = np.cos(np.pi * ft / L)
    sm = np.sin(np.pi * ft / L)
    nyq = np.where(idx % 2 == 0, 1.0, -1.0)
    sm_n = sm.copy()
    sm_n[0, :] = nyq
    fwd = np.concatenate([cm, sm_n], axis=0)
    wc = np.full((L,), 2.0)
    wc[0] = 1.0
    inv_c = cm.T * wc[None, :]
    inv_s = sm.T * 2.0
    inv_s[:, 0] = nyq
    inv = np.concatenate([inv_c, inv_s], axis=1) / (2 * L)
    return cm, sm, nyq, fwd, inv


def _filter_tables(L):
    t = np.arange(L, dtype=np.float64) / L
    bands = np.arange(1, FILTER_BANDS + 1, dtype=np.float64)
    ph = 2 * np.pi * t[:, None] * bands[None]
    feats = np.zeros((L, LANES))
    feats[:, :FILTER_EMB] = np.concatenate([t[:, None], np.sin(ph), np.cos(ph)], -1)
    max_decay = math.log(DECAY_TARGET) / DECAY_FAST_PCT
    min_decay = math.log(DECAY_TARGET) / DECAY_SLOW_PCT
    deltas = np.abs(np.linspace(min_decay, max_decay, D))
    decay = np.exp(-t[:, None] * deltas[None])
    return feats, decay


def _dot_hi(a, b):
    return jnp.dot(a, b, preferred_element_type=f32, precision=lax.Precision.HIGHEST)


def _split_bf16(a):
    hi = a.astype(bf16)
    return hi, (a - hi.astype(f32)).astype(bf16)


def _dot_split(t_hi, t_lo, b):
    b_hi, b_lo = _split_bf16(b)
    return (jnp.dot(t_hi, b_hi, preferred_element_type=f32) + jnp.dot(t_hi, b_lo, preferred_element_type=f32)
            + jnp.dot(t_lo, b_hi, preferred_element_type=f32))


def _filter_kernel(feat_ref, w1_ref, b1_ref, w2_ref, b2_ref, w3f_ref, w3b_ref, dec_ref, cmh_ref, cml_ref, smh_ref,
                   sml_ref, nyq_ref, o_ref, a_scr):
    @pl.when((pl.program_id(0) == 0) & (pl.program_id(1) == 0))
    def _():
        a1 = jnp.sin(_dot_hi(feat_ref[...], w1_ref[...]) + b1_ref[...])
        a_scr[...] = jnp.sin(_dot_hi(a1, w2_ref[...]) + b2_ref[...])

    a = a_scr[...]
    dec = dec_ref[...]
    fwd = _dot_hi(a, w3f_ref[...]) * dec
    bwd = _dot_hi(a, w3b_ref[...]) * dec
    row = lax.broadcasted_iota(jnp.int32, fwd.shape, 0)
    bwd = jnp.where(row == 0, 0.0, bwd)
    norm = jnp.sum(jnp.abs(fwd), axis=0, keepdims=True) + jnp.sum(jnp.abs(bwd), axis=0, keepdims=True) + 1e-6
    even = (fwd + bwd) / norm
    odd = (bwd - fwd) / norm
    hr = _dot_split(cmh_ref[...], cml_ref[...], even)
    hi = _dot_split(smh_ref[...], sml_ref[...], odd)
    h_nyq = jnp.sum(even * nyq_ref[...], axis=0, keepdims=True)
    o_ref[0] = hr
    o_ref[1] = jnp.where(row == 0, 0.0, hi)
    o_ref[2] = jnp.where(row == 0, h_nyq, hr)


def _hyena_filter(L, j, w1p, b1p, w2p, b2p, w3p, name):
    cm, sm, nyq, _, _ = _dft_tables(L)
    feats, decay = _filter_tables(L)
    td = 512
    nd = D // td
    full = lambda shape: pl.BlockSpec(shape, lambda o, d: (0,) * len(shape))
    return pl.pallas_call(
        _filter_kernel,
        grid=(2, nd),
        in_specs=[
            full((L, LANES)),
            pl.BlockSpec((None, LANES, LANES), lambda o, d: (j, 0, 0)),
            pl.BlockSpec((None, 1, LANES), lambda o, d: (j, 0, 0)),
            pl.BlockSpec((None, LANES, LANES), lambda o, d: (j, 0, 0)),
            pl.BlockSpec((None, 1, LANES), lambda o, d: (j, 0, 0)),
            pl.BlockSpec((None, LANES, td), lambda o, d: (j, 0, (2 * o) * nd + d)),
            pl.BlockSpec((None, LANES, td), lambda o, d: (j, 0, (2 * o + 1) * nd + d)),
            pl.BlockSpec((L, td), lambda o, d: (0, d)),
            full((L, L)), full((L, L)), full((L, L)), full((L, L)),
            full((L, 1)),
        ],
        out_specs=pl.BlockSpec((3, L, td), lambda o, d: (o, 0, d)),
        out_shape=jax.ShapeDtypeStruct((6, L, D), f32),
        scratch_shapes=[pltpu.VMEM((L, LANES), f32)],
        compiler_params=_params("arbitrary", "arbitrary"),
        name=name,
    )(jnp.asarray(feats, f32), w1p, b1p, w2p, b2p, w3p, w3p, jnp.asarray(decay, f32),
      *_np_split_bf16(cm), *_np_split_bf16(sm), jnp.asarray(nyq[:, None], f32))


def _np_split_bf16(t):
    hi = t.astype(bf16)
    lo = (t - hi.astype(np.float64)).astype(bf16)
    return jnp.asarray(hi), jnp.asarray(lo)


def _short_conv(u, w, b):
    L = u.shape[0]
    row = lax.broadcasted_iota(jnp.int32, u.shape, 0)
    prev = jnp.where(row == 0, 0.0, pltpu.roll(u, 1, 0))
    nxt = jnp.where(row == L - 1, 0.0, pltpu.roll(u, L - 1, 0))
    return prev * w[0:1, :] + u * w[1:2, :] + nxt * w[2:3, :] + b


def _spectral_product(zs, h_ref, o):
    L = zs.shape[0] // 2
    zr, zi = zs[:L], zs[L:]
    hr, hi_m, hr_n = h_ref[3 * o], h_ref[3 * o + 1], h_ref[3 * o + 2]
    return jnp.concatenate([zr * hr + zi * hi_m, zi * hr_n - zr * hi_m], axis=0).astype(bf16)


def _hyena_conv_kernel(pv_ref, p1_ref, p2_ref, wv_ref, w1_ref, w2_ref, bv_ref, b1_ref, b2_ref, h_ref, d_ref,
                       fwd_ref, inv_ref, o_ref):
    L = h_ref.shape[1]
    seqs = [slice(s * L, (s + 1) * L) for s in range(pv_ref.shape[0] // L)]
    dft = lambda z: jnp.dot(fwd_ref[...], z.astype(bf16), preferred_element_type=f32)
    idft = lambda y: jnp.dot(inv_ref[...], y, preferred_element_type=f32)
    z = [_short_conv(pv_ref[r, :].astype(f32), wv_ref[...], bv_ref[...]) for r in seqs]
    gate_in = [(p1_ref, w1_ref, b1_ref), (p2_ref, w2_ref, b2_ref)]
    for o in range(2):
        zs = [dft(zz) for zz in z]
        ys = [_spectral_product(s, h_ref, o) for s in zs]
        y = [idft(s) for s in ys]
        p_ref, w_ref, b_ref = gate_in[o]
        gates = [_short_conv(p_ref[r, :].astype(f32), w_ref[...], b_ref[...]) for r in seqs]
        z = [g * (yy + zz * d_ref[o:o + 1, :]) for g, yy, zz in zip(gates, y, z)]
    for r, zz in zip(seqs, z):
        o_ref[r, :] = zz.astype(bf16)


def _hyena_conv(p, conv_w, conv_b, spec, hy_d, j, L, nb, nseq, row0, td, name):
    _, _, _, fwd, inv = _dft_tables(L)
    nd = D // td
    once = pl.Buffered(1)
    in_specs = []
    for c in range(3):
        in_specs.append(pl.BlockSpec((nseq * L, td), lambda d, b, c=c: (row0 // nseq + b, c * nd + d)))
    for c in range(3):
        in_specs.append(pl.BlockSpec((None, 3, td), lambda d, b, c=c: (j, 0, c * nd + d)))
    for c in range(3):
        in_specs.append(pl.BlockSpec((None, 1, td), lambda d, b, c=c: (j, 0, c * nd + d)))
    in_specs += [
        pl.BlockSpec((6, L, td), lambda d, b: (0, 0, d), pipeline_mode=once),
        pl.BlockSpec((None, 2, td), lambda d, b: (j, 0, d)),
        pl.BlockSpec((2 * L, L), lambda d, b: (0, 0), pipeline_mode=once),
        pl.BlockSpec((L, 2 * L), lambda d, b: (0, 0), pipeline_mode=once),
    ]
    return pl.pallas_call(
        _hyena_conv_kernel,
        grid=(nd, nb // nseq),
        in_specs=in_specs,
        out_specs=pl.BlockSpec((nseq * L, td), lambda d, b: (b, d)),
        out_shape=jax.ShapeDtypeStruct((nb * L, D), bf16),
        compiler_params=_params("parallel", "parallel"),
        name=name,
    )(p, p, p, conv_w, conv_w, conv_w, conv_b, conv_b, conv_b, spec, hy_d,
      jnp.asarray(fwd, f32).astype(bf16), jnp.asarray(inv, f32).astype(bf16))


def _pad2(a, rows, cols):
    return jnp.pad(a, [(0, 0)] * (a.ndim - 2) + [(0, rows - a.shape[-2]), (0, cols - a.shape[-1])])


def kernel(x_prompt, x_sample, cache_k, cache_v, c, c_ctx, ada_w, ada_b, ln_g, ln_b, ffn_w1, ffn_w2, attn_w_qkv,
           attn_w_o, attn_sink, hy_w_in, hy_conv_w, hy_conv_b, hy_f_w1, hy_f_b1, hy_f_w2, hy_f_b2, hy_f_w3, hy_d,
           hy_w_out):
    cond =jnp.concatenate([c_ctx[None], c, jnp.zeros((N_COND - 1 - DEC_BATCH, D), f32)], axis=0)
    mod = _ada(cond, ada_w, ada_b)

    w1 = ffn_w1.astype(bf16)
    w2 = ffn_w2.astype(bf16)
    w_qkv = attn_w_qkv.astype(bf16)
    wqT = jnp.swapaxes(w_qkv[:, :, :Q_DIM], 1, 2)
    wkv = w_qkv[:, :, Q_DIM:]
    wvT = jnp.swapaxes(w_qkv[:, :, Q_DIM + KV_DIM:], 1, 2)
    w_o = attn_w_o.astype(bf16)
    w_in = hy_w_in.astype(bf16)
    w_out = hy_w_out.astype(bf16)
    ck = cache_k.reshape(DEC_BATCH, -1, SEQ, KV_DIM)
    cv = cache_v.reshape(DEC_BATCH, -1, SEQ, KV_DIM)
    rope = _rope_tables()
    fw1 = _pad2(hy_f_w1, LANES, LANES)
    fb1 = _pad2(hy_f_b1[:, None, :], 1, LANES)
    fw2 = _pad2(hy_f_w2, LANES, LANES)
    fb2 = _pad2(hy_f_b2[:, None, :], 1, LANES)
    fw3 = _pad2(hy_f_w3, LANES, hy_f_w3.shape[-1])
    conv_b = hy_conv_b[:, None, :]
    ln_g = ln_g.reshape(DEPTH * 3, 1, D)
    ln_b = ln_b.reshape(DEPTH * 3, 1, D)

    new_k, new_v = [], []
    xs = (x_prompt.reshape(T_CTX, D), x_sample.reshape(T_LAT, D))
    for i in range(DEPTH):
        j = i // 2
        x = _ffn(xs, mod, w1, w2, ln_g, ln_b, i, 0)
        if i % 2 == 0:
            qT_c, kv_c, vT_c = _qkv_ctx(x, mod, wqT, wkv, wvT, i, j)
            qT_l, k_l, vT_l = _qkv_lat(x, mod, wqT, wkv, wvT, rope, i, j)
            new_k.append(kv_c[:, :KV_DIM].reshape(BATCH, SEQ, KV_HEADS, HEAD_DIM))
            new_v.append(kv_c[:, KV_DIM:].reshape(BATCH, SEQ, KV_HEADS, HEAD_DIM))
            sink = attn_sink[j]
            mixer = (_attn_ctx(qT_c, kv_c, vT_c, sink), _attn_lat(qT_l, k_l, vT_l, ck, cv, sink, j), w_o, j)
        else:
            p = _modmm(x, mod, w_in, i, j, f"hy_in_l{i}")
            spec_ctx = _hyena_filter(SEQ, j, fw1, fb1, fw2, fb2, fw3, f"hy_filter_ctx_l{i}")
            spec_lat = _hyena_filter(DEC_SEQ, j, fw1, fb1, fw2, fb2, fw3, f"hy_filter_lat_l{i}")
            z_c = _hyena_conv(p, hy_conv_w, conv_b, spec_ctx, hy_d, j, SEQ, BATCH, 4, 0, 512, f"hy_conv_ctx_l{i}")
            z_l = _hyena_conv(p, hy_conv_w, conv_b, spec_lat, hy_d, j, DEC_SEQ, DEC_BATCH, 2, T_CTX // DEC_SEQ, 256,
                              f"hy_conv_lat_l{i}")
            mixer = (z_c, z_l, w_out, j)
        if i < DEPTH - 1:
            xs = (_ffn((x,), mod, w1, w2, ln_g, ln_b, i, 1, mixer=mixer),)
    y_prompt = _ffn((x,), mod, w1, w2, ln_g, ln_b, DEPTH - 1, 1, mixer=mixer, tiles=(0, N_CTX_TILES),
                    name="ffn_last_ctx")
    y_sample = _ffn((x,), mod, w1, w2, ln_g, ln_b, DEPTH - 1, 1, mixer=mixer, tiles=(N_CTX_TILES, N_TILES),
                    name="ffn_last_lat")
    return (y_prompt.reshape(BATCH, SEQ, D), y_sample.reshape(DEC_BATCH, DEC_SEQ, D),
            jnp.stack(new_k, axis=1), jnp.stack(new_v, axis=1))
```

```python
import functools
import math

import numpy as np
import jax
import jax.numpy as jnp
from jax import lax
from jax.experimental import pallas as pl
from jax.experimental.pallas import tpu as pltpu

D = 1024
BATCH, SEQ = 16, 256
DEC_BATCH, DEC_SEQ = 8, 1024
DEPTH = 4
N_HEADS, HEAD_DIM, KV_HEADS = 16, 64, 4
Q_PER_KV = N_HEADS // KV_HEADS
Q_DIM = N_HEADS * HEAD_DIM
KV_DIM = KV_HEADS * HEAD_DIM
GRID_W = 64
WINDOW = 128
BLOCK = 128
WIN_KEYS = BLOCK + 2 * WINDOW
ROPE_BASE = 10000.0
ATTN_SCALE = HEAD_DIM ** -0.5
D_FF = 2816
N_MOD = 9
LN_EPS = 1e-5
ALPHA = (2 * DEPTH) ** 0.25
NEG_INF = -1e30
FILTER_BANDS = 16
FILTER_EMB = 1 + 2 * FILTER_BANDS
DECAY_FAST_PCT, DECAY_SLOW_PCT, DECAY_TARGET = 0.3, 1.5, 1e-2

T_CTX = BATCH * SEQ
T_LAT = DEC_BATCH * DEC_SEQ
T_ALL = T_CTX + T_LAT
N_COND = 16
TM = 1024
FFN_TM = 512
FFN_SUB = 256
LANES = 128
VMEM_LIMIT = 56 * 1024 * 1024

f32 = jnp.float32
bf16 = jnp.bfloat16


def _cond_of_tile(i, tm=TM):
    return jnp.where(i < T_CTX // tm, 0, 1 + (i * tm - T_CTX) // DEC_SEQ)


def _params(*sem):
    return pltpu.CompilerParams(dimension_semantics=sem, vmem_limit_bytes=VMEM_LIMIT)


def _layer_norm(y, g, b):
    mu = jnp.mean(y, axis=-1, keepdims=True)
    yc = y - mu
    var = jnp.mean(yc * yc, axis=-1, keepdims=True)
    return yc * lax.rsqrt(var + LN_EPS) * g + b


def _modulate(x_ref, mod_ref, base):
    shift = mod_ref[base:base + 1, :]
    scale = mod_ref[base + 1:base + 2, :]
    return (x_ref[...] * (1.0 + scale) + shift).astype(bf16)


def _dot_nt(a, b):
    return lax.dot_general(a, b, (((1,), (1,)), ((), ())), preferred_element_type=f32)


def _ada_kernel(c_ref, w_ref, b_ref, o_ref):
    s = jax.nn.silu(c_ref[...]).astype(bf16)
    o_ref[...] = jnp.dot(s, w_ref[...].astype(bf16), preferred_element_type=f32) + b_ref[...]


def _ada(cond, ada_w, ada_b):
    tn = 2304
    nt = (N_MOD * D) // tn
    out = pl.pallas_call(
        _ada_kernel,
        grid=(DEPTH, nt),
        in_specs=[
            pl.BlockSpec((N_COND, D), lambda l, n: (0, 0)),
            pl.BlockSpec((None, D, tn), lambda l, n: (l, 0, n)),
            pl.BlockSpec((None, 1, tn), lambda l, n: (l, 0, n)),
        ],
        out_specs=pl.BlockSpec((N_COND, tn), lambda l, n: (0, l * nt + n)),
        out_shape=jax.ShapeDtypeStruct((N_COND, DEPTH * N_MOD * D), f32),
        compiler_params=_params("arbitrary", "arbitrary"),
        name="ada_mod",
    )(cond, ada_w, ada_b.reshape(DEPTH, 1, N_MOD * D))
    return out.reshape(N_COND, DEPTH, N_MOD, D)


N_CTX_TILES = T_CTX // FFN_TM
N_TILES = T_ALL // FFN_TM


def _ffn_kernel(*refs, mod_base, tile0, split_x, mixer):
    refs = list(refs)
    o_ref = refs.pop()
    x_refs = [refs.pop(0) for _ in range(2 if split_x else 1)]
    mod_ref = refs.pop(0)
    if mixer:
        zc_ref, zl_ref, wp_ref, gp_ref, bp_ref = [refs.pop(0) for _ in range(5)]
    w1_ref, w2_ref, g_ref, b_ref = refs
    is_ctx = tile0 + pl.program_id(0) < N_CTX_TILES
    shift = mod_ref[mod_base:mod_base + 1, :]
    scale = mod_ref[mod_base + 1:mod_base + 2, :]
    half_gate = 0.5 * mod_ref[mod_base + 2:mod_base + 3, :]
    for r in range(FFN_TM // FFN_SUB):
        rows = slice(r * FFN_SUB, (r + 1) * FFN_SUB)
        x = x_refs[0][rows, :]
        if split_x:
            x = jnp.where(is_ctx, x, x_refs[1][rows, :])
        if mixer:
            z = jnp.where(is_ctx, zc_ref[rows, :], zl_ref[rows, :])
            f = jnp.dot(z, wp_ref[...], preferred_element_type=f32)
            x = _layer_norm(ALPHA * x + mod_ref[5:6, :] * f, gp_ref[...], bp_ref[...])
        h = (x * (1.0 + scale) + shift).astype(bf16)
        gu = jnp.dot(h, w1_ref[...], preferred_element_type=f32)
        a = (jax.nn.silu(gu[:, :D_FF]) * gu[:, D_FF:]).astype(bf16)
        f = jnp.dot(a, w2_ref[...], preferred_element_type=f32)
        o_ref[rows, :] = _layer_norm(ALPHA * x + half_gate * f, g_ref[...], b_ref[...])


def _ffn(xs, mod, w1, w2, ln_g, ln_b, layer, half, mixer=None, tiles=(0, N_TILES), name=None):
    tile0, tile1 = tiles
    resident = pl.Buffered(1)
    ctx_tile = lambda i: (jnp.minimum(tile0 + i, N_CTX_TILES - 1), 0)
    lat_tile = lambda i: (jnp.maximum(tile0 + i - N_CTX_TILES, 0), 0)
    row = lambda k: pl.BlockSpec((None, 1, D), lambda i: (3 * layer + k, 0, 0))
    if len(xs) == 2:
        in_specs = [pl.BlockSpec((FFN_TM, D), ctx_tile), pl.BlockSpec((FFN_TM, D), lat_tile)]
    else:
        in_specs = [pl.BlockSpec((FFN_TM, D), lambda i: (tile0 + i, 0))]
    args = list(xs)
    in_specs.append(pl.BlockSpec((None, None, N_MOD, D), lambda i: (_cond_of_tile(tile0 + i, FFN_TM), layer, 0, 0)))
    args.append(mod)
    if mixer is not None:
        zc, zl, wp, j = mixer
        in_specs += [pl.BlockSpec((FFN_TM, D), ctx_tile), pl.BlockSpec((FFN_TM, D), lat_tile),
                     pl.BlockSpec((None, D, D), lambda i: (j, 0, 0), pipeline_mode=resident), row(1), row(1)]
        args += [zc, zl, wp, ln_g, ln_b]
    in_specs += [
        pl.BlockSpec((None, None, D, 2 * D_FF), lambda i: (layer, half, 0, 0), pipeline_mode=resident),
        pl.BlockSpec((None, None, D_FF, D), lambda i: (layer, half, 0, 0), pipeline_mode=resident),
        row(2 * half), row(2 * half),
    ]
    args += [w1, w2, ln_g, ln_b]
    return pl.pallas_call(
        functools.partial(_ffn_kernel, mod_base=6 * half, tile0=tile0, split_x=len(xs) == 2,
                          mixer=mixer is not None),
        grid=(tile1 - tile0,),
        in_specs=in_specs,
        out_specs=pl.BlockSpec((FFN_TM, D), lambda i: (i, 0)),
        out_shape=jax.ShapeDtypeStruct(((tile1 - tile0) * FFN_TM, D), f32),
        compiler_params=_params("parallel"),
        name=name or f"ffn_l{layer}h{half}",
    )(*args)


def _modmm_kernel(x_ref, mod_ref, w_ref, o_ref):
    o_ref[...] = jnp.dot(_modulate(x_ref, mod_ref, 3), w_ref[...], preferred_element_type=f32).astype(o_ref.dtype)


def _modmm(x, mod, w, layer, j, name):
    n = w.shape[-1]
    tn = 1536
    return pl.pallas_call(
        _modmm_kernel,
        grid=(T_ALL // TM, n // tn),
        in_specs=[
            pl.BlockSpec((TM, D), lambda i, c: (i, 0)),
            pl.BlockSpec((None, None, N_MOD, D), lambda i, c: (_cond_of_tile(i), layer, 0, 0)),
            pl.BlockSpec((None, D, tn), lambda i, c: (j, 0, c)),
        ],
        out_specs=pl.BlockSpec((TM, tn), lambda i, c: (i, c)),
        out_shape=jax.ShapeDtypeStruct((T_ALL, n), bf16),
        compiler_params=_params("parallel", "arbitrary"),
        name=name,
    )(x, mod, w)


def _rope_tables():
    t = np.arange(DEC_SEQ)
    pos = np.stack([t // GRID_W, t % GRID_W], axis=1).astype(np.float64)
    lane = np.arange(LANES)
    hl = lane % HEAD_DIM
    axis = hl // (HEAD_DIM // 2)
    half = HEAD_DIM // 4
    inv = ROPE_BASE ** (-(hl % half).astype(np.float64) / half)
    ang = pos[:, axis] * inv[None, :]
    first = (hl % (HEAD_DIM // 2)) < half
    cos = np.cos(ang)
    sin = np.where(first[None, :], -np.sin(ang), np.sin(ang))
    return (jnp.asarray(cos, f32), jnp.asarray(sin, f32),
            jnp.asarray(cos[:, :HEAD_DIM].T, f32), jnp.asarray(sin[:, :HEAD_DIM].T, f32))


def _rope_rows(x, cos, sin):
    lane = lax.broadcasted_iota(jnp.int32, (x.shape[0], LANES), 1)
    first = (lane % (HEAD_DIM // 2)) < (HEAD_DIM // 4)
    cols = []
    for c in range(x.shape[1] // LANES):
        xc = x[:, c * LANES:(c + 1) * LANES]
        partner = jnp.where(first, pltpu.roll(xc, LANES - HEAD_DIM // 4, 1), pltpu.roll(xc, HEAD_DIM // 4, 1))
        cols.append(xc * cos + partner * sin)
    return jnp.concatenate(cols, axis=1)


def _rope_cols(xT, cosT, sinT):
    q = HEAD_DIM // 4
    heads = []
    for h in range(xT.shape[0] // HEAD_DIM):
        xh = xT[h * HEAD_DIM:(h + 1) * HEAD_DIM, :]
        partner = jnp.concatenate([xh[q:2 * q], xh[0:q], xh[3 * q:4 * q], xh[2 * q:3 * q]], axis=0)
        heads.append(xh * cosT + partner * sinT)
    return jnp.concatenate(heads, axis=0)


def _qkv_ctx_kernel(x_ref, mod_ref, wqT_ref, wkv_ref, wvT_ref, qT_ref, kv_ref, vT_ref):
    h = _modulate(x_ref, mod_ref, 3)
    qT_ref[...] = (_dot_nt(wqT_ref[...], h) * ATTN_SCALE).astype(bf16)
    kv_ref[...] = jnp.dot(h, wkv_ref[...], preferred_element_type=f32)
    vT_ref[...] = _dot_nt(wvT_ref[...], h).astype(bf16)


def _qkv_ctx(x, mod, wqT, wkv, wvT, layer, j):
    return pl.pallas_call(
        _qkv_ctx_kernel,
        grid=(T_CTX // TM,),
        in_specs=[
            pl.BlockSpec((TM, D), lambda i: (i, 0)),
            pl.BlockSpec((None, None, N_MOD, D), lambda i: (0, layer, 0, 0)),
            pl.BlockSpec((None, Q_DIM, D), lambda i: (j, 0, 0)),
            pl.BlockSpec((None, D, 2 * KV_DIM), lambda i: (j, 0, 0)),
            pl.BlockSpec((None, KV_DIM, D), lambda i: (j, 0, 0)),
        ],
        out_specs=[
            pl.BlockSpec((Q_DIM, TM), lambda i: (0, i)),
            pl.BlockSpec((TM, 2 * KV_DIM), lambda i: (i, 0)),
            pl.BlockSpec((KV_DIM, TM), lambda i: (0, i)),
        ],
        out_shape=[
            jax.ShapeDtypeStruct((Q_DIM, T_CTX), bf16),
            jax.ShapeDtypeStruct((T_CTX, 2 * KV_DIM), f32),
            jax.ShapeDtypeStruct((KV_DIM, T_CTX), bf16),
        ],
        compiler_params=_params("parallel"),
        name=f"qkv_ctx_l{layer}",
    )(x, mod, wqT, wkv, wvT)


def _qkv_lat_kernel(x_ref, mod_ref, wqT_ref, wkv_ref, wvT_ref, cos_ref, sin_ref, cosT_ref, sinT_ref,
                    qT_ref, k_ref, vT_ref):
    h = _modulate(x_ref, mod_ref, 3)
    qT = _rope_cols(_dot_nt(wqT_ref[...], h), cosT_ref[...], sinT_ref[...])
    qT_ref[...] = (qT * ATTN_SCALE).astype(bf16)
    k = jnp.dot(h, wkv_ref[:, :KV_DIM], preferred_element_type=f32)
    k_ref[...] = _rope_rows(k, cos_ref[...], sin_ref[...]).astype(bf16)
    vT_ref[...] = _dot_nt(wvT_ref[...], h).astype(bf16)


def _qkv_lat(x, mod, wqT, wkv, wvT, rope, layer, j):
    row0 = T_CTX // DEC_SEQ
    full = lambda shape: pl.BlockSpec(shape, lambda b: (0, 0))
    return pl.pallas_call(
        _qkv_lat_kernel,
        grid=(DEC_BATCH,),
        in_specs=[
            pl.BlockSpec((DEC_SEQ, D), lambda b: (row0 + b, 0)),
            pl.BlockSpec((None, None, N_MOD, D), lambda b: (1 + b, layer, 0, 0)),
            pl.BlockSpec((None, Q_DIM, D), lambda b: (j, 0, 0)),
            pl.BlockSpec((None, D, 2 * KV_DIM), lambda b: (j, 0, 0)),
            pl.BlockSpec((None, KV_DIM, D), lambda b: (j, 0, 0)),
            full((DEC_SEQ, LANES)), full((DEC_SEQ, LANES)), full((HEAD_DIM, DEC_SEQ)), full((HEAD_DIM, DEC_SEQ)),
        ],
        out_specs=[
            pl.BlockSpec((Q_DIM, DEC_SEQ), lambda b: (0, b)),
            pl.BlockSpec((DEC_SEQ, KV_DIM), lambda b: (b, 0)),
            pl.BlockSpec((KV_DIM, DEC_SEQ), lambda b: (0, b)),
        ],
        out_shape=[
            jax.ShapeDtypeStruct((Q_DIM, T_LAT), bf16),
            jax.ShapeDtypeStruct((T_LAT, KV_DIM), bf16),
            jax.ShapeDtypeStruct((KV_DIM, T_LAT), bf16),
        ],
        compiler_params=_params("parallel"),
        name=f"qkv_lat_l{layer}",
    )(x, mod, wqT, wkv, wvT, *rope)


def _softmax_t(scores, sink_row):
    m = sink_row
    for s in scores:
        m = jnp.maximum(m, jnp.max(s, axis=0, keepdims=True))
    denom = jnp.exp(sink_row - m)
    probs = []
    for s in scores:
        p = jnp.exp(s - m)
        denom = denom + jnp.sum(p, axis=0, keepdims=True)
        probs.append(p.astype(bf16))
    return probs, denom


def _attend(scores, values, sink_rows):
    soft = [_softmax_t(s, sink) for s, sink in zip(scores, sink_rows)]
    outs = []
    for (probs, denom), vals in zip(soft, values):
        o = None
        for p, vT in zip(probs, vals):
            pv = jnp.dot(vT, p, preferred_element_type=f32)
            o = pv if o is None else o + pv
        outs.append(o / denom)
    return outs


def _group_queries(qT_ref, g):
    heads = [qT_ref[(Q_PER_KV * g + i) * HEAD_DIM:(Q_PER_KV * g + i + 1) * HEAD_DIM, :] for i in range(Q_PER_KV)]
    return jnp.concatenate(heads, axis=1)


def _sink_row(sink_ref, g, nq):
    return jnp.concatenate([jnp.full((1, nq), sink_ref[Q_PER_KV * g + i], f32) for i in range(Q_PER_KV)], axis=1)


def _store_heads(o_ref, group_outs, nq):
    heads = [oT[:, i * nq:(i + 1) * nq] for oT in group_outs for i in range(Q_PER_KV)]
    o_ref[...] = jnp.concatenate(heads, axis=0).T.astype(o_ref.dtype)


def _attn_ctx_kernel(sink_ref, qT_ref, kv_ref, vT_ref, o_ref):
    k = kv_ref[:, :KV_DIM].astype(bf16)
    groups = [slice(g * HEAD_DIM, (g + 1) * HEAD_DIM) for g in range(KV_HEADS)]
    scores = [[jnp.dot(k[:, gs], _group_queries(qT_ref, g), preferred_element_type=f32)]
              for g, gs in enumerate(groups)]
    values = [[vT_ref[gs, :]] for gs in groups]
    sinks = [_sink_row(sink_ref, g, SEQ) for g in range(KV_HEADS)]
    _store_heads(o_ref, _attend(scores, values, sinks), SEQ)


def _attn_ctx(qT, kv, vT, sink):
    return pl.pallas_call(
        _attn_ctx_kernel,
        grid=(BATCH,),
        in_specs=[
            pl.BlockSpec(memory_space=pltpu.SMEM),
            pl.BlockSpec((Q_DIM, SEQ), lambda b: (0, b)),
            pl.BlockSpec((SEQ, 2 * KV_DIM), lambda b: (b, 0)),
            pl.BlockSpec((KV_DIM, SEQ), lambda b: (0, b)),
        ],
        out_specs=pl.BlockSpec((SEQ, D), lambda b: (b, 0)),
        out_shape=jax.ShapeDtypeStruct((T_CTX, D), bf16),
        compiler_params=_params("parallel"),
        name="attn_ctx",
    )(sink, qT, kv, vT)


def _attn_lat_kernel(sink_ref, qT_ref, kp_ref, kc_ref, kn_ref, vp_ref, vc_ref, vn_ref, ck_ref, cv_ref, o_ref,
                     cvT_scr):
    qb = pl.program_id(1)

    @pl.when(qb == 0)
    def _():
        cvT_scr[...] = cv_ref[...].T.astype(bf16)

    n = Q_PER_KV * BLOCK
    row = lax.broadcasted_iota(jnp.int32, (WIN_KEYS, n), 0)
    ql = lax.broadcasted_iota(jnp.int32, (WIN_KEYS, n), 1) & (BLOCK - 1)
    kj = (qb - 1) * BLOCK + row
    rel = row - ql
    valid = (rel >= 0) & (rel <= 2 * WINDOW) & (kj >= 0) & (kj < DEC_SEQ)
    k_win = jnp.concatenate([kp_ref[...], kc_ref[...], kn_ref[...]], axis=0)
    k_ctx = ck_ref[...].astype(bf16)
    groups = [slice(g * HEAD_DIM, (g + 1) * HEAD_DIM) for g in range(KV_HEADS)]
    scores, values = [], []
    for g, gs in enumerate(groups):
        qg = _group_queries(qT_ref, g)
        s_loc = jnp.where(valid, jnp.dot(k_win[:, gs], qg, preferred_element_type=f32), NEG_INF)
        s_ctx = jnp.dot(k_ctx[:, gs], qg, preferred_element_type=f32)
        scores.append([s_loc, s_ctx])
        v_win = jnp.concatenate([vp_ref[gs, :], vc_ref[gs, :], vn_ref[gs, :]], axis=1)
        values.append([v_win, cvT_scr[gs, :]])
    sinks = [_sink_row(sink_ref, g, BLOCK) for g in range(KV_HEADS)]
    _store_heads(o_ref, _attend(scores, values, sinks), BLOCK)


def _attn_lat(qT, k, vT, cache_k, cache_v, sink, j):
    nqb = DEC_SEQ // BLOCK
    prev = lambda b, i: b * nqb + jnp.maximum(i - 1, 0)
    own = lambda b, i: b * nqb + i
    nxt = lambda b, i: b * nqb + jnp.minimum(i + 1, nqb - 1)
    return pl.pallas_call(
        _attn_lat_kernel,
        grid=(DEC_BATCH, nqb),
        in_specs=[
            pl.BlockSpec(memory_space=pltpu.SMEM),
            pl.BlockSpec((Q_DIM, BLOCK), lambda b, i: (0, own(b, i))),
            pl.BlockSpec((BLOCK, KV_DIM), lambda b, i: (prev(b, i), 0)),
            pl.BlockSpec((BLOCK, KV_DIM), lambda b, i: (own(b, i), 0)),
            pl.BlockSpec((BLOCK, KV_DIM), lambda b, i: (nxt(b, i), 0)),
            pl.BlockSpec((KV_DIM, BLOCK), lambda b, i: (0, prev(b, i))),
            pl.BlockSpec((KV_DIM, BLOCK), lambda b, i: (0, own(b, i))),
            pl.BlockSpec((KV_DIM, BLOCK), lambda b, i: (0, nxt(b, i))),
            pl.BlockSpec((None, None, SEQ, KV_DIM), lambda b, i: (b, j, 0, 0)),
            pl.BlockSpec((None, None, SEQ, KV_DIM), lambda b, i: (b, j, 0, 0)),
        ],
        out_specs=pl.BlockSpec((BLOCK, D), lambda b, i: (own(b, i), 0)),
        out_shape=jax.ShapeDtypeStruct((T_LAT, D), bf16),
        scratch_shapes=[pltpu.VMEM((KV_DIM, SEQ), bf16)],
        compiler_params=_params("parallel", "arbitrary"),
        name="attn_lat",
    )(sink, qT, k, k, k, vT, vT, vT, cache_k, cache_v)


def _dft_tables(L):
    idx = np.arange(L)
    ft = np.outer(idx, idx) % (2 * L)
    cm = np.cos(np.pi * ft / L)
    sm = np.sin(np.pi * ft / L)
    nyq = np.where(idx % 2 == 0, 1.0, -1.0)
    sm_n = sm.copy()
    sm_n[0, :] = nyq
    fwd = np.concatenate([cm, sm_n], axis=0)
    wc = np.full((L,), 2.0)
    wc[0] = 1.0
    inv_c = cm.T * wc[None, :]
    inv_s = sm.T * 2.0
    inv_s[:, 0] = nyq
    inv = np.concatenate([inv_c, inv_s], axis=1) / (2 * L)
    return cm, sm, nyq, fwd, inv


def _filter_tables(L):
    t = np.arange(L, dtype=np.float64) / L
    bands = np.arange(1, FILTER_BANDS + 1, dtype=np.float64)
    ph = 2 * np.pi * t[:, None] * bands[None]
    feats = np.zeros((L, LANES))
    feats[:, :FILTER_EMB] = np.concatenate([t[:, None], np.sin(ph), np.cos(ph)], -1)
    max_decay = math.log(DECAY_TARGET) / DECAY_FAST_PCT
    min_decay = math.log(DECAY_TARGET) / DECAY_SLOW_PCT
    deltas = np.abs(np.linspace(min_decay, max_decay, D))
    decay = np.exp(-t[:, None] * deltas[None])
    return feats, decay


def _dot_hi(a, b):
    return jnp.dot(a, b, preferred_element_type=f32, precision=lax.Precision.HIGHEST)


def _split_bf16(a):
    hi = a.astype(bf16)
    return hi, (a - hi.astype(f32)).astype(bf16)


def _dot_split(t_hi, t_lo, b):
    b_hi, b_lo = _split_bf16(b)
    return (jnp.dot(t_hi, b_hi, preferred_element_type=f32) + jnp.dot(t_hi, b_lo, preferred_element_type=f32)
            + jnp.dot(t_lo, b_hi, preferred_element_type=f32))


def _filter_kernel(feat_ref, w1_ref, b1_ref, w2_ref, b2_ref, w3f_ref, w3b_ref, dec_ref, cmh_ref, cml_ref, smh_ref,
                   sml_ref, nyq_ref, o_ref, a_scr):
    @pl.when((pl.program_id(0) == 0) & (pl.program_id(1) == 0))
    def _():
        a1 = jnp.sin(_dot_hi(feat_ref[...], w1_ref[...]) + b1_ref[...])
        a_scr[...] = jnp.sin(_dot_hi(a1, w2_ref[...]) + b2_ref[...])

    a = a_scr[...]
    dec = dec_ref[...]
    fwd = _dot_hi(a, w3f_ref[...]) * dec
    bwd = _dot_hi(a, w3b_ref[...]) * dec
    row = lax.broadcasted_iota(jnp.int32, fwd.shape, 0)
    bwd = jnp.where(row == 0, 0.0, bwd)
    norm = jnp.sum(jnp.abs(fwd), axis=0, keepdims=True) + jnp.sum(jnp.abs(bwd), axis=0, keepdims=True) + 1e-6
    even = (fwd + bwd) / norm
    odd = (bwd - fwd) / norm
    hr = _dot_split(cmh_ref[...], cml_ref[...], even)
    hi = _dot_split(smh_ref[...], sml_ref[...], odd)
    h_nyq = jnp.sum(even * nyq_ref[...], axis=0, keepdims=True)
    o_ref[0] = hr
    o_ref[1] = jnp.where(row == 0, 0.0, hi)
    o_ref[2] = jnp.where(row == 0, h_nyq, hr)


def _hyena_filter(L, j, w1p, b1p, w2p, b2p, w3p, name):
    cm, sm, nyq, _, _ = _dft_tables(L)
    feats, decay = _filter_tables(L)
    td = 512
    nd = D // td
    full = lambda shape: pl.BlockSpec(shape, lambda o, d: (0,) * len(shape))
    return pl.pallas_call(
        _filter_kernel,
        grid=(2, nd),
        in_specs=[
            full((L, LANES)),
            pl.BlockSpec((None, LANES, LANES), lambda o, d: (j, 0, 0)),
            pl.BlockSpec((None, 1, LANES), lambda o, d: (j, 0, 0)),
            pl.BlockSpec((None, LANES, LANES), lambda o, d: (j, 0, 0)),
            pl.BlockSpec((None, 1, LANES), lambda o, d: (j, 0, 0)),
            pl.BlockSpec((None, LANES, td), lambda o, d: (j, 0, (2 * o) * nd + d)),
            pl.BlockSpec((None, LANES, td), lambda o, d: (j, 0, (2 * o + 1) * nd + d)),
            pl.BlockSpec((L, td), lambda o, d: (0, d)),
            full((L, L)), full((L, L)), full((L, L)), full((L, L)),
            full((L, 1)),
        ],
        out_specs=pl.BlockSpec((3, L, td), lambda o, d: (o, 0, d)),
        out_shape=jax.ShapeDtypeStruct((6, L, D), f32),
        scratch_shapes=[pltpu.VMEM((L, LANES), f32)],
        compiler_params=_params("arbitrary", "arbitrary"),
        name=name,
    )(jnp.asarray(feats, f32), w1p, b1p, w2p, b2p, w3p, w3p, jnp.asarray(decay, f32),
      *_np_split_bf16(cm), *_np_split_bf16(sm), jnp.asarray(nyq[:, None], f32))


def _np_split_bf16(t):
    hi = t.astype(bf16)
    lo = (t - hi.astype(np.float64)).astype(bf16)
    return jnp.asarray(hi), jnp.asarray(lo)


def _short_conv(u, w, b):
    L = u.shape[0]
    row = lax.broadcasted_iota(jnp.int32, u.shape, 0)
    prev = jnp.where(row == 0, 0.0, pltpu.roll(u, 1, 0))
    nxt = jnp.where(row == L - 1, 0.0, pltpu.roll(u, L - 1, 0))
    return prev * w[0:1, :] + u * w[1:2, :] + nxt * w[2:3, :] + b


def _spectral_product(zs, h_ref, o):
    L = zs.shape[0] // 2
    zr, zi = zs[:L], zs[L:]
    hr, hi_m, hr_n = h_ref[3 * o], h_ref[3 * o + 1], h_ref[3 * o + 2]
    return jnp.concatenate([zr * hr + zi * hi_m, zi * hr_n - zr * hi_m], axis=0).astype(bf16)


def _hyena_conv_kernel(pv_ref, p1_ref, p2_ref, wv_ref, w1_ref, w2_ref, bv_ref, b1_ref, b2_ref, h_ref, d_ref,
                       fwd_ref, inv_ref, o_ref):
    L = h_ref.shape[1]
    seqs = [slice(s * L, (s + 1) * L) for s in range(pv_ref.shape[0] // L)]
    dft = lambda z: jnp.dot(fwd_ref[...], z.astype(bf16), preferred_element_type=f32)
    idft = lambda y: jnp.dot(inv_ref[...], y, preferred_element_type=f32)
    z = [_short_conv(pv_ref[r, :].astype(f32), wv_ref[...], bv_ref[...]) for r in seqs]
    gate_in = [(p1_ref, w1_ref, b1_ref), (p2_ref, w2_ref, b2_ref)]
    for o in range(2):
        zs = [dft(zz) for zz in z]
        ys = [_spectral_product(s, h_ref, o) for s in zs]
        y = [idft(s) for s in ys]
        p_ref, w_ref, b_ref = gate_in[o]
        gates = [_short_conv(p_ref[r, :].astype(f32), w_ref[...], b_ref[...]) for r in seqs]
        z = [g * (yy + zz * d_ref[o:o + 1, :]) for g, yy, zz in zip(gates, y, z)]
    for r, zz in zip(seqs, z):
        o_ref[r, :] = zz.astype(bf16)


def _hyena_conv(p, conv_w, conv_b, spec, hy_d, j, L, nb, nseq, row0, td, name):
    _, _, _, fwd, inv = _dft_tables(L)
    nd = D // td
    once = pl.Buffered(1)
    in_specs = []
    for c in range(3):
        in_specs.append(pl.BlockSpec((nseq * L, td), lambda d, b, c=c: (row0 // nseq + b, c * nd + d)))
    for c in range(3):
        in_specs.append(pl.BlockSpec((None, 3, td), lambda d, b, c=c: (j, 0, c * nd + d)))
    for c in range(3):
        in_specs.append(pl.BlockSpec((None, 1, td), lambda d, b, c=c: (j, 0, c * nd + d)))
    in_specs += [
        pl.BlockSpec((6, L, td), lambda d, b: (0, 0, d), pipeline_mode=once),
        pl.BlockSpec((None, 2, td), lambda d, b: (j, 0, d)),
        pl.BlockSpec((2 * L, L), lambda d, b: (0, 0), pipeline_mode=once),
        pl.BlockSpec((L, 2 * L), lambda d, b: (0, 0), pipeline_mode=once),
    ]
    return pl.pallas_call(
        _hyena_conv_kernel,
        grid=(nd, nb // nseq),
        in_specs=in_specs,
        out_specs=pl.BlockSpec((nseq * L, td), lambda d, b: (b, d)),
        out_shape=jax.ShapeDtypeStruct((nb * L, D), bf16),
        compiler_params=_params("parallel", "parallel"),
        name=name,
    )(p, p, p, conv_w, conv_w, conv_w, conv_b, conv_b, conv_b, spec, hy_d,
      jnp.asarray(fwd, f32).astype(bf16), jnp.asarray(inv, f32).astype(bf16))


def _pad2(a, rows, cols):
    return jnp.pad(a, [(0, 0)] * (a.ndim - 2) + [(0, rows - a.shape[-2]), (0, cols - a.shape[-1])])


def kernel(x_prompt, x_sample, cache_k, cache_v, c, c_ctx, ada_w, ada_b, ln_g, ln_b, ffn_w1, ffn_w2, attn_w_qkv,
           attn_w_o, attn_sink, hy_w_in, hy_conv_w, hy_conv_b, hy_f_w1, hy_f_b1, hy_f_w2, hy_f_b2, hy_f_w3, hy_d,
           hy_w_out):
    cond =jnp.concatenate([c_ctx[None], c, jnp.zeros((N_COND - 1 - DEC_BATCH, D), f32)], axis=0)
    mod = _ada(cond, ada_w, ada_b)

    w1 = ffn_w1.astype(bf16)
    w2 = ffn_w2.astype(bf16)
    w_qkv = attn_w_qkv.astype(bf16)
    wqT = jnp.swapaxes(w_qkv[:, :, :Q_DIM], 1, 2)
    wkv = w_qkv[:, :, Q_DIM:]
    wvT = jnp.swapaxes(w_qkv[:, :, Q_DIM + KV_DIM:], 1, 2)
    w_o = attn_w_o.astype(bf16)
    w_in = hy_w_in.astype(bf16)
    w_out = hy_w_out.astype(bf16)
    ck = cache_k.reshape(DEC_BATCH, -1, SEQ, KV_DIM)
    cv = cache_v.reshape(DEC_BATCH, -1, SEQ, KV_DIM)
    rope = _rope_tables()
    fw1 = _pad2(hy_f_w1, LANES, LANES)
    fb1 = _pad2(hy_f_b1[:, None, :], 1, LANES)
    fw2 = _pad2(hy_f_w2, LANES, LANES)
    fb2 = _pad2(hy_f_b2[:, None, :], 1, LANES)
    fw3 = _pad2(hy_f_w3, LANES, hy_f_w3.shape[-1])
    conv_b = hy_conv_b[:, None, :]
    ln_g = ln_g.reshape(DEPTH * 3, 1, D)
    ln_b = ln_b.reshape(DEPTH * 3, 1, D)

    new_k, new_v = [], []
    xs = (x_prompt.reshape(T_CTX, D), x_sample.reshape(T_LAT, D))
    for i in range(DEPTH):
        j = i // 2
        x = _ffn(xs, mod, w1, w2, ln_g, ln_b, i, 0)
        if i % 2 == 0:
            qT_c, kv_c, vT_c = _qkv_ctx(x, mod, wqT, wkv, wvT, i, j)
            qT_l, k_l, vT_l = _qkv_lat(x, mod, wqT, wkv, wvT, rope, i, j)
            new_k.append(kv_c[:, :KV_DIM].reshape(BATCH, SEQ, KV_HEADS, HEAD_DIM))
            new_v.append(kv_c[:, KV_DIM:].reshape(BATCH, SEQ, KV_HEADS, HEAD_DIM))
            sink = attn_sink[j]
            mixer = (_attn_ctx(qT_c, kv_c, vT_c, sink), _attn_lat(qT_l, k_l, vT_l, ck, cv, sink, j), w_o, j)
        else:
            p = _modmm(x, mod, w_in, i, j, f"hy_in_l{i}")
            spec_ctx = _hyena_filter(SEQ, j, fw1, fb1, fw2, fb2, fw3, f"hy_filter_ctx_l{i}")
            spec_lat = _hyena_filter(DEC_SEQ, j, fw1, fb1, fw2, fb2, fw3, f"hy_filter_lat_l{i}")
            z_c = _hyena_conv(p, hy_conv_w, conv_b, spec_ctx, hy_d, j, SEQ, BATCH, 4, 0, 512, f"hy_conv_ctx_l{i}")
            z_l = _hyena_conv(p, hy_conv_w, conv_b, spec_lat, hy_d, j, DEC_SEQ, DEC_BATCH, 2, T_CTX // DEC_SEQ, 256,
                              f"hy_conv_lat_l{i}")
            mixer = (z_c, z_l, w_out, j)
        if i < DEPTH - 1:
            xs = (_ffn((x,), mod, w1, w2, ln_g, ln_b, i, 1, mixer=mixer),)
    y_prompt = _ffn((x,), mod, w1, w2, ln_g, ln_b, DEPTH - 1, 1, mixer=mixer, tiles=(0, N_CTX_TILES),
                    name="ffn_last_ctx")
    y_sample = _ffn((x,), mod, w1, w2, ln_g, ln_b, DEPTH - 1, 1, mixer=mixer, tiles=(N_CTX_TILES, N_TILES),
                    name="ffn_last_lat")
    return (y_prompt.reshape(BATCH, SEQ, D), y_sample.reshape(DEC_BATCH, DEC_SEQ, D),
            jnp.stack(new_k, axis=1), jnp.stack(new_v, axis=1))
```

```python
import functools
import math

import numpy as np
import jax
import jax.numpy as jnp
from jax import lax
from jax.experimental import pallas as pl
from jax.experimental.pallas import tpu as pltpu

D = 1024
BATCH, SEQ = 16, 256
DEC_BATCH, DEC_SEQ = 8, 1024
DEPTH = 4
N_HEADS, HEAD_DIM, KV_HEADS = 16, 64, 4
Q_PER_KV = N_HEADS // KV_HEADS
Q_DIM = N_HEADS * HEAD_DIM
KV_DIM = KV_HEADS * HEAD_DIM
GRID_W = 64
WINDOW = 128
BLOCK = 128
WIN_KEYS = BLOCK + 2 * WINDOW
ROPE_BASE = 10000.0
ATTN_SCALE = HEAD_DIM ** -0.5
D_FF = 2816
N_MOD = 9
LN_EPS = 1e-5
ALPHA = (2 * DEPTH) ** 0.25
NEG_INF = -1e30
FILTER_BANDS = 16
FILTER_EMB = 1 + 2 * FILTER_BANDS
DECAY_FAST_PCT, DECAY_SLOW_PCT, DECAY_TARGET = 0.3, 1.5, 1e-2

T_CTX = BATCH * SEQ
T_LAT = DEC_BATCH * DEC_SEQ
T_ALL = T_CTX + T_LAT
N_COND = 16
TM = 1024
FFN_TM = 512
FFN_SUB = 256
LANES = 128
VMEM_LIMIT = 56 * 1024 * 1024

f32 = jnp.float32
bf16 = jnp.bfloat16


def _cond_of_tile(i, tm=TM):
    return jnp.where(i < T_CTX // tm, 0, 1 + (i * tm - T_CTX) // DEC_SEQ)


def _params(*sem):
    return pltpu.CompilerParams(dimension_semantics=sem, vmem_limit_bytes=VMEM_LIMIT)


def _layer_norm(y, g, b):
    mu = jnp.mean(y, axis=-1, keepdims=True)
    yc = y - mu
    var = jnp.mean(yc * yc, axis=-1, keepdims=True)
    return yc * lax.rsqrt(var + LN_EPS) * g + b


def _modulate(x_ref, mod_ref, base):
    shift = mod_ref[base:base + 1, :]
    scale = mod_ref[base + 1:base + 2, :]
    return (x_ref[...] * (1.0 + scale) + shift).astype(bf16)


def _dot_nt(a, b):
    return lax.dot_general(a, b, (((1,), (1,)), ((), ())), preferred_element_type=f32)


def _ada_kernel(c_ref, w_ref, b_ref, o_ref):
    s = jax.nn.silu(c_ref[...]).astype(bf16)
    o_ref[...] = jnp.dot(s, w_ref[...].astype(bf16), preferred_element_type=f32) + b_ref[...]


def _ada(cond, ada_w, ada_b):
    tn = 2304
    nt = (N_MOD * D) // tn
    out = pl.pallas_call(
        _ada_kernel,
        grid=(DEPTH, nt),
        in_specs=[
            pl.BlockSpec((N_COND, D), lambda l, n: (0, 0)),
            pl.BlockSpec((None, D, tn), lambda l, n: (l, 0, n)),
            pl.BlockSpec((None, 1, tn), lambda l, n: (l, 0, n)),
        ],
        out_specs=pl.BlockSpec((N_COND, tn), lambda l, n: (0, l * nt + n)),
        out_shape=jax.ShapeDtypeStruct((N_COND, DEPTH * N_MOD * D), f32),
        compiler_params=_params("arbitrary", "arbitrary"),
        name="ada_mod",
    )(cond, ada_w, ada_b.reshape(DEPTH, 1, N_MOD * D))
    return out.reshape(N_COND, DEPTH, N_MOD, D)


N_CTX_TILES = T_CTX // FFN_TM
N_TILES = T_ALL // FFN_TM


def _ffn_kernel(*refs, mod_base, tile0, split_x, mixer):
    refs = list(refs)
    o_ref = refs.pop()
    x_refs = [refs.pop(0) for _ in range(2 if split_x else 1)]
    mod_ref = refs.pop(0)
    if mixer:
        zc_ref, zl_ref, wp_ref, gp_ref, bp_ref = [refs.pop(0) for _ in range(5)]
    w1_ref, w2_ref, g_ref, b_ref = refs
    is_ctx = tile0 + pl.program_id(0) < N_CTX_TILES
    shift = mod_ref[mod_base:mod_base + 1, :]
    scale = mod_ref[mod_base + 1:mod_base + 2, :]
    half_gate = 0.5 * mod_ref[mod_base + 2:mod_base + 3, :]
    subs = [slice(r * FFN_SUB, (r + 1) * FFN_SUB) for r in range(FFN_TM // FFN_SUB)]
    xs = [x_refs[0][rows, :] for rows in subs]
    if split_x:
        xs = [jnp.where(is_ctx, x, x_refs[1][rows, :]) for x, rows in zip(xs, subs)]
    if mixer:
        zs = [jnp.where(is_ctx, zc_ref[rows, :], zl_ref[rows, :]) for rows in subs]
        fs = [jnp.dot(z, wp_ref[...], preferred_element_type=f32) for z in zs]
        xs = [_layer_norm(ALPHA * x + mod_ref[5:6, :] * f, gp_ref[...], bp_ref[...]) for x, f in zip(xs, fs)]
    hs = [(x * (1.0 + scale) + shift).astype(bf16) for x in xs]
    gus = [jnp.dot(h, w1_ref[...], preferred_element_type=f32) for h in hs]
    acts = [(jax.nn.silu(gu[:, :D_FF]) * gu[:, D_FF:]).astype(bf16) for gu in gus]
    fs = [jnp.dot(a, w2_ref[...], preferred_element_type=f32) for a in acts]
    for rows, x, f in zip(subs, xs, fs):
        o_ref[rows, :] = _layer_norm(ALPHA * x + half_gate * f, g_ref[...], b_ref[...])


def _ffn(xs, mod, w1, w2, ln_g, ln_b, layer, half, mixer=None, tiles=(0, N_TILES), name=None):
    tile0, tile1 = tiles
    resident = pl.Buffered(1)
    ctx_tile = lambda i: (jnp.minimum(tile0 + i, N_CTX_TILES - 1), 0)
    lat_tile = lambda i: (jnp.maximum(tile0 + i - N_CTX_TILES, 0), 0)
    row = lambda k: pl.BlockSpec((None, 1, D), lambda i: (3 * layer + k, 0, 0))
    if len(xs) == 2:
        in_specs = [pl.BlockSpec((FFN_TM, D), ctx_tile), pl.BlockSpec((FFN_TM, D), lat_tile)]
    else:
        in_specs = [pl.BlockSpec((FFN_TM, D), lambda i: (tile0 + i, 0))]
    args = list(xs)
    in_specs.append(pl.BlockSpec((None, None, N_MOD, D), lambda i: (_cond_of_tile(tile0 + i, FFN_TM), layer, 0, 0)))
    args.append(mod)
    if mixer is not None:
        zc, zl, wp, j = mixer
        in_specs += [pl.BlockSpec((FFN_TM, D), ctx_tile), pl.BlockSpec((FFN_TM, D), lat_tile),
                     pl.BlockSpec((None, D, D), lambda i: (j, 0, 0), pipeline_mode=resident), row(1), row(1)]
        args += [zc, zl, wp, ln_g, ln_b]
    in_specs += [
        pl.BlockSpec((None, None, D, 2 * D_FF), lambda i: (layer, half, 0, 0), pipeline_mode=resident),
        pl.BlockSpec((None, None, D_FF, D), lambda i: (layer, half, 0, 0), pipeline_mode=resident),
        row(2 * half), row(2 * half),
    ]
    args += [w1, w2, ln_g, ln_b]
    return pl.pallas_call(
        functools.partial(_ffn_kernel, mod_base=6 * half, tile0=tile0, split_x=len(xs) == 2,
                          mixer=mixer is not None),
        grid=(tile1 - tile0,),
        in_specs=in_specs,
        out_specs=pl.BlockSpec((FFN_TM, D), lambda i: (i, 0)),
        out_shape=jax.ShapeDtypeStruct(((tile1 - tile0) * FFN_TM, D), f32),
        compiler_params=_params("parallel"),
        name=name or f"ffn_l{layer}h{half}",
    )(*args)


def _modmm_kernel(x_ref, mod_ref, w_ref, o_ref):
    o_ref[...] = jnp.dot(_modulate(x_ref, mod_ref, 3), w_ref[...], preferred_element_type=f32).astype(o_ref.dtype)


def _modmm(x, mod, w, layer, j, name):
    n = w.shape[-1]
    tn = 1536
    return pl.pallas_call(
        _modmm_kernel,
        grid=(T_ALL // TM, n // tn),
        in_specs=[
            pl.BlockSpec((TM, D), lambda i, c: (i, 0)),
            pl.BlockSpec((None, None, N_MOD, D), lambda i, c: (_cond_of_tile(i), layer, 0, 0)),
            pl.BlockSpec((None, D, tn), lambda i, c: (j, 0, c)),
        ],
        out_specs=pl.BlockSpec((TM, tn), lambda i, c: (i, c)),
        out_shape=jax.ShapeDtypeStruct((T_ALL, n), bf16),
        compiler_params=_params("parallel", "arbitrary"),
        name=name,
    )(x, mod, w)


def _rope_tables():
    t = np.arange(DEC_SEQ)
    pos = np.stack([t // GRID_W, t % GRID_W], axis=1).astype(np.float64)
    lane = np.arange(LANES)
    hl = lane % HEAD_DIM
    axis = hl // (HEAD_DIM // 2)
    half = HEAD_DIM // 4
    inv = ROPE_BASE ** (-(hl % half).astype(np.float64) / half)
    ang = pos[:, axis] * inv[None, :]
    first = (hl % (HEAD_DIM // 2)) < half
    cos = np.cos(ang)
    sin = np.where(first[None, :], -np.sin(ang), np.sin(ang))
    return (jnp.asarray(cos, f32), jnp.asarray(sin, f32),
            jnp.asarray(cos[:, :HEAD_DIM].T, f32), jnp.asarray(sin[:, :HEAD_DIM].T, f32))


def _rope_rows(x, cos, sin):
    lane = lax.broadcasted_iota(jnp.int32, (x.shape[0], LANES), 1)
    first = (lane % (HEAD_DIM // 2)) < (HEAD_DIM // 4)
    cols = []
    for c in range(x.shape[1] // LANES):
        xc = x[:, c * LANES:(c + 1) * LANES]
        partner = jnp.where(first, pltpu.roll(xc, LANES - HEAD_DIM // 4, 1), pltpu.roll(xc, HEAD_DIM // 4, 1))
        cols.append(xc * cos + partner * sin)
    return jnp.concatenate(cols, axis=1)


def _rope_cols(xT, cosT, sinT):
    q = HEAD_DIM // 4
    heads = []
    for h in range(xT.shape[0] // HEAD_DIM):
        xh = xT[h * HEAD_DIM:(h + 1) * HEAD_DIM, :]
        partner = jnp.concatenate([xh[q:2 * q], xh[0:q], xh[3 * q:4 * q], xh[2 * q:3 * q]], axis=0)
        heads.append(xh * cosT + partner * sinT)
    return jnp.concatenate(heads, axis=0)


def _qkv_ctx_kernel(x_ref, mod_ref, wqT_ref, wkv_ref, wvT_ref, qT_ref, kv_ref, vT_ref):
    h = _modulate(x_ref, mod_ref, 3)
    qT_ref[...] = (_dot_nt(wqT_ref[...], h) * ATTN_SCALE).astype(bf16)
    kv_ref[...] = jnp.dot(h, wkv_ref[...], preferred_element_type=f32)
    vT_ref[...] = _dot_nt(wvT_ref[...], h).astype(bf16)


def _qkv_ctx(x, mod, wqT, wkv, wvT, layer, j):
    return pl.pallas_call(
        _qkv_ctx_kernel,
        grid=(T_CTX // TM,),
        in_specs=[
            pl.BlockSpec((TM, D), lambda i: (i, 0)),
            pl.BlockSpec((None, None, N_MOD, D), lambda i: (0, layer, 0, 0)),
            pl.BlockSpec((None, Q_DIM, D), lambda i: (j, 0, 0)),
            pl.BlockSpec((None, D, 2 * KV_DIM), lambda i: (j, 0, 0)),
            pl.BlockSpec((None, KV_DIM, D), lambda i: (j, 0, 0)),
        ],
        out_specs=[
            pl.BlockSpec((Q_DIM, TM), lambda i: (0, i)),
            pl.BlockSpec((TM, 2 * KV_DIM), lambda i: (i, 0)),
            pl.BlockSpec((KV_DIM, TM), lambda i: (0, i)),
        ],
        out_shape=[
            jax.ShapeDtypeStruct((Q_DIM, T_CTX), bf16),
            jax.ShapeDtypeStruct((T_CTX, 2 * KV_DIM), f32),
            jax.ShapeDtypeStruct((KV_DIM, T_CTX), bf16),
        ],
        compiler_params=_params("parallel"),
        name=f"qkv_ctx_l{layer}",
    )(x, mod, wqT, wkv, wvT)


def _qkv_lat_kernel(x_ref, mod_ref, wqT_ref, wkv_ref, wvT_ref, cos_ref, sin_ref, cosT_ref, sinT_ref,
                    qT_ref, k_ref, vT_ref):
    h = _modulate(x_ref, mod_ref, 3)
    qT = _rope_cols(_dot_nt(wqT_ref[...], h), cosT_ref[...], sinT_ref[...])
    qT_ref[...] = (qT * ATTN_SCALE).astype(bf16)
    k = jnp.dot(h, wkv_ref[:, :KV_DIM], preferred_element_type=f32)
    k_ref[...] = _rope_rows(k, cos_ref[...], sin_ref[...]).astype(bf16)
    vT_ref[...] = _dot_nt(wvT_ref[...], h).astype(bf16)


def _qkv_lat(x, mod, wqT, wkv, wvT, rope, layer, j):
    row0 = T_CTX // DEC_SEQ
    full = lambda shape: pl.BlockSpec(shape, lambda b: (0, 0))
    return pl.pallas_call(
        _qkv_lat_kernel,
        grid=(DEC_BATCH,),
        in_specs=[
            pl.BlockSpec((DEC_SEQ, D), lambda b: (row0 + b, 0)),
            pl.BlockSpec((None, None, N_MOD, D), lambda b: (1 + b, layer, 0, 0)),
            pl.BlockSpec((None, Q_DIM, D), lambda b: (j, 0, 0)),
            pl.BlockSpec((None, D, 2 * KV_DIM), lambda b: (j, 0, 0)),
            pl.BlockSpec((None, KV_DIM, D), lambda b: (j, 0, 0)),
            full((DEC_SEQ, LANES)), full((DEC_SEQ, LANES)), full((HEAD_DIM, DEC_SEQ)), full((HEAD_DIM, DEC_SEQ)),
        ],
        out_specs=[
            pl.BlockSpec((Q_DIM, DEC_SEQ), lambda b: (0, b)),
            pl.BlockSpec((DEC_SEQ, KV_DIM), lambda b: (b, 0)),
            pl.BlockSpec((KV_DIM, DEC_SEQ), lambda b: (0, b)),
        ],
        out_shape=[
            jax.ShapeDtypeStruct((Q_DIM, T_LAT), bf16),
            jax.ShapeDtypeStruct((T_LAT, KV_DIM), bf16),
            jax.ShapeDtypeStruct((KV_DIM, T_LAT), bf16),
        ],
        compiler_params=_params("parallel"),
        name=f"qkv_lat_l{layer}",
    )(x, mod, wqT, wkv, wvT, *rope)


def _softmax_t(scores, sink_row):
    m = sink_row
    for s in scores:
        m = jnp.maximum(m, jnp.max(s, axis=0, keepdims=True))
    denom = jnp.exp(sink_row - m)
    probs = []
    for s in scores:
        p = jnp.exp(s - m)
        denom = denom + jnp.sum(p, axis=0, keepdims=True)
        probs.append(p.astype(bf16))
    return probs, denom


def _attend(scores, values, sink_rows):
    soft = [_softmax_t(s, sink) for s, sink in zip(scores, sink_rows)]
    outs = []
    for (probs, denom), vals in zip(soft, values):
        o = None
        for p, vT in zip(probs, vals):
            pv = jnp.dot(vT, p, preferred_element_type=f32)
            o = pv if o is None else o + pv
        outs.append(o / denom)
    return outs


def _group_queries(qT_ref, g):
    heads = [qT_ref[(Q_PER_KV * g + i) * HEAD_DIM:(Q_PER_KV * g + i + 1) * HEAD_DIM, :] for i in range(Q_PER_KV)]
    return jnp.concatenate(heads, axis=1)


def _sink_row(sink_ref, g, nq):
    return jnp.concatenate([jnp.full((1, nq), sink_ref[Q_PER_KV * g + i], f32) for i in range(Q_PER_KV)], axis=1)


def _store_heads(o_ref, group_outs, nq):
    heads = [oT[:, i * nq:(i + 1) * nq] for oT in group_outs for i in range(Q_PER_KV)]
    o_ref[...] = jnp.concatenate(heads, axis=0).T.astype(o_ref.dtype)


def _attn_ctx_kernel(sink_ref, qT_ref, kv_ref, vT_ref, o_ref):
    k = kv_ref[:, :KV_DIM].astype(bf16)
    groups = [slice(g * HEAD_DIM, (g + 1) * HEAD_DIM) for g in range(KV_HEADS)]
    scores = [[jnp.dot(k[:, gs], _group_queries(qT_ref, g), preferred_element_type=f32)]
              for g, gs in enumerate(groups)]
    values = [[vT_ref[gs, :]] for gs in groups]
    sinks = [_sink_row(sink_ref, g, SEQ) for g in range(KV_HEADS)]
    _store_heads(o_ref, _attend(scores, values, sinks), SEQ)


def _attn_ctx(qT, kv, vT, sink):
    return pl.pallas_call(
        _attn_ctx_kernel,
        grid=(BATCH,),
        in_specs=[
            pl.BlockSpec(memory_space=pltpu.SMEM),
            pl.BlockSpec((Q_DIM, SEQ), lambda b: (0, b)),
            pl.BlockSpec((SEQ, 2 * KV_DIM), lambda b: (b, 0)),
            pl.BlockSpec((KV_DIM, SEQ), lambda b: (0, b)),
        ],
        out_specs=pl.BlockSpec((SEQ, D), lambda b: (b, 0)),
        out_shape=jax.ShapeDtypeStruct((T_CTX, D), bf16),
        compiler_params=_params("parallel"),
        name="attn_ctx",
    )(sink, qT, kv, vT)


def _attn_lat_kernel(sink_ref, qT_ref, kp_ref, kc_ref, kn_ref, vp_ref, vc_ref, vn_ref, ck_ref, cv_ref, o_ref,
                     cvT_scr):
    qb = pl.program_id(1)

    @pl.when(qb == 0)
    def _():
        cvT_scr[...] = cv_ref[...].T.astype(bf16)

    n = Q_PER_KV * BLOCK
    row = lax.broadcasted_iota(jnp.int32, (WIN_KEYS, n), 0)
    ql = lax.broadcasted_iota(jnp.int32, (WIN_KEYS, n), 1) & (BLOCK - 1)
    kj = (qb - 1) * BLOCK + row
    rel = row - ql
    valid = (rel >= 0) & (rel <= 2 * WINDOW) & (kj >= 0) & (kj < DEC_SEQ)
    k_win = jnp.concatenate([kp_ref[...], kc_ref[...], kn_ref[...]], axis=0)
    k_ctx = ck_ref[...].astype(bf16)
    groups = [slice(g * HEAD_DIM, (g + 1) * HEAD_DIM) for g in range(KV_HEADS)]
    scores, values = [], []
    for g, gs in enumerate(groups):
        qg = _group_queries(qT_ref, g)
        s_loc = jnp.where(valid, jnp.dot(k_win[:, gs], qg, preferred_element_type=f32), NEG_INF)
        s_ctx = jnp.dot(k_ctx[:, gs], qg, preferred_element_type=f32)
        scores.append([s_loc, s_ctx])
        v_win = jnp.concatenate([vp_ref[gs, :], vc_ref[gs, :], vn_ref[gs, :]], axis=1)
        values.append([v_win, cvT_scr[gs, :]])
    sinks = [_sink_row(sink_ref, g, BLOCK) for g in range(KV_HEADS)]
    _store_heads(o_ref, _attend(scores, values, sinks), BLOCK)


def _attn_lat(qT, k, vT, cache_k, cache_v, sink, j):
    nqb = DEC_SEQ // BLOCK
    prev = lambda b, i: b * nqb + jnp.maximum(i - 1, 0)
    own = lambda b, i: b * nqb + i
    nxt = lambda b, i: b * nqb + jnp.minimum(i + 1, nqb - 1)
    return pl.pallas_call(
        _attn_lat_kernel,
        grid=(DEC_BATCH, nqb),
        in_specs=[
            pl.BlockSpec(memory_space=pltpu.SMEM),
            pl.BlockSpec((Q_DIM, BLOCK), lambda b, i: (0, own(b, i))),
            pl.BlockSpec((BLOCK, KV_DIM), lambda b, i: (prev(b, i), 0)),
            pl.BlockSpec((BLOCK, KV_DIM), lambda b, i: (own(b, i), 0)),
            pl.BlockSpec((BLOCK, KV_DIM), lambda b, i: (nxt(b, i), 0)),
            pl.BlockSpec((KV_DIM, BLOCK), lambda b, i: (0, prev(b, i))),
            pl.BlockSpec((KV_DIM, BLOCK), lambda b, i: (0, own(b, i))),
            pl.BlockSpec((KV_DIM, BLOCK), lambda b, i: (0, nxt(b, i))),
            pl.BlockSpec((None, None, SEQ, KV_DIM), lambda b, i: (b, j, 0, 0)),
            pl.BlockSpec((None, None, SEQ, KV_DIM), lambda b, i: (b, j, 0, 0)),
        ],
        out_specs=pl.BlockSpec((BLOCK, D), lambda b, i: (own(b, i), 0)),
        out_shape=jax.ShapeDtypeStruct((T_LAT, D), bf16),
        scratch_shapes=[pltpu.VMEM((KV_DIM, SEQ), bf16)],
        compiler_params=_params("parallel", "arbitrary"),
        name="attn_lat",
    )(sink, qT, k, k, k, vT, vT, vT, cache_k, cache_v)


def _dft_tables(L):
    idx = np.arange(L)
    ft = np.outer(idx, idx) % (2 * L)
    cm = np.cos(np.pi * ft / L)
    sm = np.sin(np.pi * ft / L)
    nyq = np.where(idx % 2 == 0, 1.0, -1.0)
    sm_n = sm.copy()
    sm_n[0, :] = nyq
    fwd = np.concatenate([cm, sm_n], axis=0)
    wc = np.full((L,), 2.0)
    wc[0] = 1.0
    inv_c = cm.T * wc[None, :]
    inv_s = sm.T * 2.0
    inv_s[:, 0] = nyq
    inv = np.concatenate([inv_c, inv_s], axis=1) / (2 * L)
    return cm, sm, nyq, fwd, inv


def _filter_tables(L):
    t = np.arange(L, dtype=np.float64) / L
    bands = np.arange(1, FILTER_BANDS + 1, dtype=np.float64)
    ph = 2 * np.pi * t[:, None] * bands[None]
    feats = np.zeros((L, LANES))
    feats[:, :FILTER_EMB] = np.concatenate([t[:, None], np.sin(ph), np.cos(ph)], -1)
    max_decay = math.log(DECAY_TARGET) / DECAY_FAST_PCT
    min_decay = math.log(DECAY_TARGET) / DECAY_SLOW_PCT
    deltas = np.abs(np.linspace(min_decay, max_decay, D))
    decay = np.exp(-t[:, None] * deltas[None])
    return feats, decay


def _dot_hi(a, b):
    return jnp.dot(a, b, preferred_element_type=f32, precision=lax.Precision.HIGHEST)


def _split_bf16(a):
    hi = a.astype(bf16)
    return hi, (a - hi.astype(f32)).astype(bf16)


def _dot_split(t_hi, t_lo, b):
    b_hi, b_lo = _split_bf16(b)
    return (jnp.dot(t_hi, b_hi, preferred_element_type=f32) + jnp.dot(t_hi, b_lo, preferred_element_type=f32)
            + jnp.dot(t_lo, b_hi, preferred_element_type=f32))


def _filter_kernel(feat_ref, w1_ref, b1_ref, w2_ref, b2_ref, w3f_ref, w3b_ref, dec_ref, cmh_ref, cml_ref, smh_ref,
                   sml_ref, nyq_ref, o_ref, a_scr):
    @pl.when((pl.program_id(0) == 0) & (pl.program_id(1) == 0))
    def _():
        a1 = jnp.sin(_dot_hi(feat_ref[...], w1_ref[...]) + b1_ref[...])
        a_scr[...] = jnp.sin(_dot_hi(a1, w2_ref[...]) + b2_ref[...])

    a = a_scr[...]
    dec = dec_ref[...]
    fwd = _dot_hi(a, w3f_ref[...]) * dec
    bwd = _dot_hi(a, w3b_ref[...]) * dec
    row = lax.broadcasted_iota(jnp.int32, fwd.shape, 0)
    bwd = jnp.where(row == 0, 0.0, bwd)
    norm = jnp.sum(jnp.abs(fwd), axis=0, keepdims=True) + jnp.sum(jnp.abs(bwd), axis=0, keepdims=True) + 1e-6
    even = (fwd + bwd) / norm
    odd = (bwd - fwd) / norm
    hr = _dot_split(cmh_ref[...], cml_ref[...], even)
    hi = _dot_split(smh_ref[...], sml_ref[...], odd)
    h_nyq = jnp.sum(even * nyq_ref[...], axis=0, keepdims=True)
    o_ref[0] = hr
    o_ref[1] = jnp.where(row == 0, 0.0, hi)
    o_ref[2] = jnp.where(row == 0, h_nyq, hr)


def _hyena_filter(L, j, w1p, b1p, w2p, b2p, w3p, name):
    cm, sm, nyq, _, _ = _dft_tables(L)
    feats, decay = _filter_tables(L)
    td = 512
    nd = D // td
    full = lambda shape: pl.BlockSpec(shape, lambda o, d: (0,) * len(shape))
    return pl.pallas_call(
        _filter_kernel,
        grid=(2, nd),
        in_specs=[
            full((L, LANES)),
            pl.BlockSpec((None, LANES, LANES), lambda o, d: (j, 0, 0)),
            pl.BlockSpec((None, 1, LANES), lambda o, d: (j, 0, 0)),
            pl.BlockSpec((None, LANES, LANES), lambda o, d: (j, 0, 0)),
            pl.BlockSpec((None, 1, LANES), lambda o, d: (j, 0, 0)),
            pl.BlockSpec((None, LANES, td), lambda o, d: (j, 0, (2 * o) * nd + d)),
            pl.BlockSpec((None, LANES, td), lambda o, d: (j, 0, (2 * o + 1) * nd + d)),
            pl.BlockSpec((L, td), lambda o, d: (0, d)),
            full((L, L)), full((L, L)), full((L, L)), full((L, L)),
            full((L, 1)),
        ],
        out_specs=pl.BlockSpec((3, L, td), lambda o, d: (o, 0, d)),
        out_shape=jax.ShapeDtypeStruct((6, L, D), f32),
        scratch_shapes=[pltpu.VMEM((L, LANES), f32)],
        compiler_params=_params("arbitrary", "arbitrary"),
        name=name,
    )(jnp.asarray(feats, f32), w1p, b1p, w2p, b2p, w3p, w3p, jnp.asarray(decay, f32),
      *_np_split_bf16(cm), *_np_split_bf16(sm), jnp.asarray(nyq[:, None], f32))


def _np_split_bf16(t):
    hi = t.astype(bf16)
    lo = (t - hi.astype(np.float64)).astype(bf16)
    return jnp.asarray(hi), jnp.asarray(lo)


def _short_conv(u, w, b):
    L = u.shape[0]
    row = lax.broadcasted_iota(jnp.int32, u.shape, 0)
    prev = jnp.where(row == 0, 0.0, pltpu.roll(u, 1, 0))
    nxt = jnp.where(row == L - 1, 0.0, pltpu.roll(u, L - 1, 0))
    return prev * w[0:1, :] + u * w[1:2, :] + nxt * w[2:3, :] + b


def _spectral_product(zs, h_ref, o):
    L = zs.shape[0] // 2
    zr, zi = zs[:L], zs[L:]
    hr, hi_m, hr_n = h_ref[3 * o], h_ref[3 * o + 1], h_ref[3 * o + 2]
    return jnp.concatenate([zr * hr + zi * hi_m, zi * hr_n - zr * hi_m], axis=0).astype(bf16)


def _hyena_conv_kernel(pv_ref, p1_ref, p2_ref, wv_ref, w1_ref, w2_ref, bv_ref, b1_ref, b2_ref, h_ref, d_ref,
                       fwd_ref, inv_ref, o_ref):
    L = h_ref.shape[1]
    seqs = [slice(s * L, (s + 1) * L) for s in range(pv_ref.shape[0] // L)]
    dft = lambda z: jnp.dot(fwd_ref[...], z.astype(bf16), preferred_element_type=f32)
    idft = lambda y: jnp.dot(inv_ref[...], y, preferred_element_type=f32)
    z = [_short_conv(pv_ref[r, :].astype(f32), wv_ref[...], bv_ref[...]) for r in seqs]
    gate_in = [(p1_ref, w1_ref, b1_ref), (p2_ref, w2_ref, b2_ref)]
    for o in range(2):
        zs = [dft(zz) for zz in z]
        ys = [_spectral_product(s, h_ref, o) for s in zs]
        y = [idft(s) for s in ys]
        p_ref, w_ref, b_ref = gate_in[o]
        gates = [_short_conv(p_ref[r, :].astype(f32), w_ref[...], b_ref[...]) for r in seqs]
        z = [g * (yy + zz * d_ref[o:o + 1, :]) for g, yy, zz in zip(gates, y, z)]
    for r, zz in zip(seqs, z):
        o_ref[r, :] = zz.astype(bf16)


def _hyena_conv(p, conv_w, conv_b, spec, hy_d, j, L, nb, nseq, row0, td, name):
    _, _, _, fwd, inv = _dft_tables(L)
    nd = D // td
    once = pl.Buffered(1)
    in_specs = []
    for c in range(3):
        in_specs.append(pl.BlockSpec((nseq * L, td), lambda d, b, c=c: (row0 // nseq + b, c * nd + d)))
    for c in range(3):
        in_specs.append(pl.BlockSpec((None, 3, td), lambda d, b, c=c: (j, 0, c * nd + d)))
    for c in range(3):
        in_specs.append(pl.BlockSpec((None, 1, td), lambda d, b, c=c: (j, 0, c * nd + d)))
    in_specs += [
        pl.BlockSpec((6, L, td), lambda d, b: (0, 0, d), pipeline_mode=once),
        pl.BlockSpec((None, 2, td), lambda d, b: (j, 0, d)),
        pl.BlockSpec((2 * L, L), lambda d, b: (0, 0), pipeline_mode=once),
        pl.BlockSpec((L, 2 * L), lambda d, b: (0, 0), pipeline_mode=once),
    ]
    return pl.pallas_call(
        _hyena_conv_kernel,
        grid=(nd, nb // nseq),
        in_specs=in_specs,
        out_specs=pl.BlockSpec((nseq * L, td), lambda d, b: (b, d)),
        out_shape=jax.ShapeDtypeStruct((nb * L, D), bf16),
        compiler_params=_params("parallel", "parallel"),
        name=name,
    )(p, p, p, conv_w, conv_w, conv_w, conv_b, conv_b, conv_b, spec, hy_d,
      jnp.asarray(fwd, f32).astype(bf16), jnp.asarray(inv, f32).astype(bf16))


def _pad2(a, rows, cols):
    return jnp.pad(a, [(0, 0)] * (a.ndim - 2) + [(0, rows - a.shape[-2]), (0, cols - a.shape[-1])])


def kernel(x_prompt, x_sample, cache_k, cache_v, c, c_ctx, ada_w, ada_b, ln_g, ln_b, ffn_w1, ffn_w2, attn_w_qkv,
           attn_w_o, attn_sink, hy_w_in, hy_conv_w, hy_conv_b, hy_f_w1, hy_f_b1, hy_f_w2, hy_f_b2, hy_f_w3, hy_d,
           hy_w_out):
    cond =jnp.concatenate([c_ctx[None], c, jnp.zeros((N_COND - 1 - DEC_BATCH, D), f32)], axis=0)
    mod = _ada(cond, ada_w, ada_b)

    w1 = ffn_w1.astype(bf16)
    w2 = ffn_w2.astype(bf16)
    w_qkv = attn_w_qkv.astype(bf16)
    wqT = jnp.swapaxes(w_qkv[:, :, :Q_DIM], 1, 2)
    wkv = w_qkv[:, :, Q_DIM:]
    wvT = jnp.swapaxes(w_qkv[:, :, Q_DIM + KV_DIM:], 1, 2)
    w_o = attn_w_o.astype(bf16)
    w_in = hy_w_in.astype(bf16)
    w_out = hy_w_out.astype(bf16)
    ck = cache_k.reshape(DEC_BATCH, -1, SEQ, KV_DIM)
    cv = cache_v.reshape(DEC_BATCH, -1, SEQ, KV_DIM)
    rope = _rope_tables()
    fw1 = _pad2(hy_f_w1, LANES, LANES)
    fb1 = _pad2(hy_f_b1[:, None, :], 1, LANES)
    fw2 = _pad2(hy_f_w2, LANES, LANES)
    fb2 = _pad2(hy_f_b2[:, None, :], 1, LANES)
    fw3 = _pad2(hy_f_w3, LANES, hy_f_w3.shape[-1])
    conv_b = hy_conv_b[:, None, :]
    ln_g = ln_g.reshape(DEPTH * 3, 1, D)
    ln_b = ln_b.reshape(DEPTH * 3, 1, D)

    new_k, new_v = [], []
    xs = (x_prompt.reshape(T_CTX, D), x_sample.reshape(T_LAT, D))
    for i in range(DEPTH):
        j = i // 2
        x = _ffn(xs, mod, w1, w2, ln_g, ln_b, i, 0)
        if i % 2 == 0:
            qT_c, kv_c, vT_c = _qkv_ctx(x, mod, wqT, wkv, wvT, i, j)
            qT_l, k_l, vT_l = _qkv_lat(x, mod, wqT, wkv, wvT, rope, i, j)
            new_k.append(kv_c[:, :KV_DIM].reshape(BATCH, SEQ, KV_HEADS, HEAD_DIM))
            new_v.append(kv_c[:, KV_DIM:].reshape(BATCH, SEQ, KV_HEADS, HEAD_DIM))
            sink = attn_sink[j]
            mixer = (_attn_ctx(qT_c, kv_c, vT_c, sink), _attn_lat(qT_l, k_l, vT_l, ck, cv, sink, j), w_o, j)
        else:
            p = _modmm(x, mod, w_in, i, j, f"hy_in_l{i}")
            spec_ctx = _hyena_filter(SEQ, j, fw1, fb1, fw2, fb2, fw3, f"hy_filter_ctx_l{i}")
            spec_lat = _hyena_filter(DEC_SEQ, j, fw1, fb1, fw2, fb2, fw3, f"hy_filter_lat_l{i}")
            z_c = _hyena_conv(p, hy_conv_w, conv_b, spec_ctx, hy_d, j, SEQ, BATCH, 4, 0, 512, f"hy_conv_ctx_l{i}")
            z_l = _hyena_conv(p, hy_conv_w, conv_b, spec_lat, hy_d, j, DEC_SEQ, DEC_BATCH, 2, T_CTX // DEC_SEQ, 256,
                              f"hy_conv_lat_l{i}")
            mixer = (z_c, z_l, w_out, j)
        if i < DEPTH - 1:
            xs = (_ffn((x,), mod, w1, w2, ln_g, ln_b, i, 1, mixer=mixer),)
    y_prompt = _ffn((x,), mod, w1, w2, ln_g, ln_b, DEPTH - 1, 1, mixer=mixer, tiles=(0, N_CTX_TILES),
                    name="ffn_last_ctx")
    y_sample = _ffn((x,), mod, w1, w2, ln_g, ln_b, DEPTH - 1, 1, mixer=mixer, tiles=(N_CTX_TILES, N_TILES),
                    name="ffn_last_lat")
    return (y_prompt.reshape(BATCH, SEQ, D), y_sample.reshape(DEC_BATCH, DEC_SEQ, D),
            jnp.stack(new_k, axis=1), jnp.stack(new_v, axis=1))
```

```python
import functools
import math

import numpy as np
import jax
import jax.numpy as jnp
from jax import lax
from jax.experimental import pallas as pl
from jax.experimental.pallas import tpu as pltpu

D = 1024
BATCH, SEQ = 16, 256
DEC_BATCH, DEC_SEQ = 8, 1024
DEPTH = 4
N_HEADS, HEAD_DIM, KV_HEADS = 16, 64, 4
Q_PER_KV = N_HEADS // KV_HEADS
Q_DIM = N_HEADS * HEAD_DIM
KV_DIM = KV_HEADS * HEAD_DIM
GRID_W = 64
WINDOW = 128
BLOCK = 128
WIN_KEYS = BLOCK + 2 * WINDOW
ROPE_BASE = 10000.0
ATTN_SCALE = HEAD_DIM ** -0.5
LOG2E = math.log2(math.e)
Q_SCALE = ATTN_SCALE * LOG2E
D_FF = 2816
N_MOD = 9
LN_EPS = 1e-5
ALPHA = (2 * DEPTH) ** 0.25
NEG_INF = -1e30
FILTER_BANDS = 16
FILTER_EMB = 1 + 2 * FILTER_BANDS
DECAY_FAST_PCT, DECAY_SLOW_PCT, DECAY_TARGET = 0.3, 1.5, 1e-2

T_CTX = BATCH * SEQ
T_LAT = DEC_BATCH * DEC_SEQ
T_ALL = T_CTX + T_LAT
N_COND = 16
TM = 1024
FFN_TM = 512
FFN_SUB = 256
LANES = 128
VMEM_LIMIT = 56 * 1024 * 1024

f32 = jnp.float32
bf16 = jnp.bfloat16


def _cond_of_tile(i, tm=TM):
    return jnp.where(i < T_CTX // tm, 0, 1 + (i * tm - T_CTX) // DEC_SEQ)


def _params(*sem):
    return pltpu.CompilerParams(dimension_semantics=sem, vmem_limit_bytes=VMEM_LIMIT)


def _layer_norm(y, g, b):
    mu = jnp.mean(y, axis=-1, keepdims=True)
    yc = y - mu
    var = jnp.mean(yc * yc, axis=-1, keepdims=True)
    return yc * lax.rsqrt(var + LN_EPS) * g + b


def _modulate(x_ref, mod_ref, base):
    shift = mod_ref[base:base + 1, :]
    scale = mod_ref[base + 1:base + 2, :]
    return (x_ref[...] * (1.0 + scale) + shift).astype(bf16)


def _dot_nt(a, b):
    return lax.dot_general(a, b, (((1,), (1,)), ((), ())), preferred_element_type=f32)


def _ada_kernel(c_ref, w_ref, b_ref, o_ref):
    s = jax.nn.silu(c_ref[...]).astype(bf16)
    o_ref[...] = jnp.dot(s, w_ref[...].astype(bf16), preferred_element_type=f32) + b_ref[...]


def _ada(cond, ada_w, ada_b):
    tn = 2304
    nt = (N_MOD * D) // tn
    out = pl.pallas_call(
        _ada_kernel,
        grid=(DEPTH, nt),
        in_specs=[
            pl.BlockSpec((N_COND, D), lambda l, n: (0, 0)),
            pl.BlockSpec((None, D, tn), lambda l, n: (l, 0, n)),
            pl.BlockSpec((None, 1, tn), lambda l, n: (l, 0, n)),
        ],
        out_specs=pl.BlockSpec((N_COND, tn), lambda l, n: (0, l * nt + n)),
        out_shape=jax.ShapeDtypeStruct((N_COND, DEPTH * N_MOD * D), f32),
        compiler_params=_params("arbitrary", "arbitrary"),
        name="ada_mod",
    )(cond, ada_w, ada_b.reshape(DEPTH, 1, N_MOD * D))
    return out.reshape(N_COND, DEPTH, N_MOD, D)


N_CTX_TILES = T_CTX // FFN_TM
N_TILES = T_ALL // FFN_TM


def _ffn_kernel(*refs, mod_base, tile0, split_x, mixer):
    refs = list(refs)
    o_ref = refs.pop()
    x_refs = [refs.pop(0) for _ in range(2 if split_x else 1)]
    mod_ref = refs.pop(0)
    if mixer:
        zc_ref, zl_ref, wp_ref, gp_ref, bp_ref = [refs.pop(0) for _ in range(5)]
    w1_ref, w2_ref, g_ref, b_ref = refs
    is_ctx = tile0 + pl.program_id(0) < N_CTX_TILES
    shift = mod_ref[mod_base:mod_base + 1, :]
    scale = mod_ref[mod_base + 1:mod_base + 2, :]
    half_gate = 0.5 * mod_ref[mod_base + 2:mod_base + 3, :]
    subs = [slice(r * FFN_SUB, (r + 1) * FFN_SUB) for r in range(FFN_TM // FFN_SUB)]
    xs = [x_refs[0][rows, :] for rows in subs]
    if split_x:
        xs = [jnp.where(is_ctx, x, x_refs[1][rows, :]) for x, rows in zip(xs, subs)]
    if mixer:
        zs = [jnp.where(is_ctx, zc_ref[rows, :], zl_ref[rows, :]) for rows in subs]
        fs = [jnp.dot(z, wp_ref[...], preferred_element_type=f32) for z in zs]
        xs = [_layer_norm(ALPHA * x + mod_ref[5:6, :] * f, gp_ref[...], bp_ref[...]) for x, f in zip(xs, fs)]
    hs = [(x * (1.0 + scale) + shift).astype(bf16) for x in xs]
    gus = [jnp.dot(h, w1_ref[...], preferred_element_type=f32) for h in hs]
    acts = [(jax.nn.silu(gu[:, :D_FF]) * gu[:, D_FF:]).astype(bf16) for gu in gus]
    fs = [jnp.dot(a, w2_ref[...], preferred_element_type=f32) for a in acts]
    for rows, x, f in zip(subs, xs, fs):
        o_ref[rows, :] = _layer_norm(ALPHA * x + half_gate * f, g_ref[...], b_ref[...])


def _ffn(xs, mod, w1, w2, ln_g, ln_b, layer, half, mixer=None, tiles=(0, N_TILES), name=None):
    tile0, tile1 = tiles
    resident = pl.Buffered(1)
    ctx_tile = lambda i: (jnp.minimum(tile0 + i, N_CTX_TILES - 1), 0)
    lat_tile = lambda i: (jnp.maximum(tile0 + i - N_CTX_TILES, 0), 0)
    row = lambda k: pl.BlockSpec((None, 1, D), lambda i: (3 * layer + k, 0, 0))
    if len(xs) == 2:
        in_specs = [pl.BlockSpec((FFN_TM, D), ctx_tile), pl.BlockSpec((FFN_TM, D), lat_tile)]
    else:
        in_specs = [pl.BlockSpec((FFN_TM, D), lambda i: (tile0 + i, 0))]
    args = list(xs)
    in_specs.append(pl.BlockSpec((None, None, N_MOD, D), lambda i: (_cond_of_tile(tile0 + i, FFN_TM), layer, 0, 0)))
    args.append(mod)
    if mixer is not None:
        zc, zl, wp, j = mixer
        in_specs += [pl.BlockSpec((FFN_TM, D), ctx_tile), pl.BlockSpec((FFN_TM, D), lat_tile),
                     pl.BlockSpec((None, D, D), lambda i: (j, 0, 0), pipeline_mode=resident), row(1), row(1)]
        args += [zc, zl, wp, ln_g, ln_b]
    in_specs += [
        pl.BlockSpec((None, None, D, 2 * D_FF), lambda i: (layer, half, 0, 0), pipeline_mode=resident),
        pl.BlockSpec((None, None, D_FF, D), lambda i: (layer, half, 0, 0), pipeline_mode=resident),
        row(2 * half), row(2 * half),
    ]
    args += [w1, w2, ln_g, ln_b]
    return pl.pallas_call(
        functools.partial(_ffn_kernel, mod_base=6 * half, tile0=tile0, split_x=len(xs) == 2,
                          mixer=mixer is not None),
        grid=(tile1 - tile0,),
        in_specs=in_specs,
        out_specs=pl.BlockSpec((FFN_TM, D), lambda i: (i, 0)),
        out_shape=jax.ShapeDtypeStruct(((tile1 - tile0) * FFN_TM, D), f32),
        compiler_params=_params("parallel"),
        name=name or f"ffn_l{layer}h{half}",
    )(*args)


def _modmm_kernel(x_ref, mod_ref, w_ref, o_ref):
    o_ref[...] = jnp.dot(_modulate(x_ref, mod_ref, 3), w_ref[...], preferred_element_type=f32).astype(o_ref.dtype)


def _modmm(x, mod, w, layer, j, name):
    n = w.shape[-1]
    tn = 1536
    return pl.pallas_call(
        _modmm_kernel,
        grid=(T_ALL // TM, n // tn),
        in_specs=[
            pl.BlockSpec((TM, D), lambda i, c: (i, 0)),
            pl.BlockSpec((None, None, N_MOD, D), lambda i, c: (_cond_of_tile(i), layer, 0, 0)),
            pl.BlockSpec((None, D, tn), lambda i, c: (j, 0, c)),
        ],
        out_specs=pl.BlockSpec((TM, tn), lambda i, c: (i, c)),
        out_shape=jax.ShapeDtypeStruct((T_ALL, n), bf16),
        compiler_params=_params("parallel", "arbitrary"),
        name=name,
    )(x, mod, w)


def _rope_tables():
    t = np.arange(DEC_SEQ)
    pos = np.stack([t // GRID_W, t % GRID_W], axis=1).astype(np.float64)
    lane = np.arange(LANES)
    hl = lane % HEAD_DIM
    axis = hl // (HEAD_DIM // 2)
    half = HEAD_DIM // 4
    inv = ROPE_BASE ** (-(hl % half).astype(np.float64) / half)
    ang = pos[:, axis] * inv[None, :]
    first = (hl % (HEAD_DIM // 2)) < half
    cos = np.cos(ang)
    sin = np.where(first[None, :], -np.sin(ang), np.sin(ang))
    return (jnp.asarray(cos, f32), jnp.asarray(sin, f32),
            jnp.asarray(cos[:, :HEAD_DIM].T, f32), jnp.asarray(sin[:, :HEAD_DIM].T, f32))


def _rope_rows(x, cos, sin):
    lane = lax.broadcasted_iota(jnp.int32, (x.shape[0], LANES), 1)
    first = (lane % (HEAD_DIM // 2)) < (HEAD_DIM // 4)
    cols = []
    for c in range(x.shape[1] // LANES):
        xc = x[:, c * LANES:(c + 1) * LANES]
        partner = jnp.where(first, pltpu.roll(xc, LANES - HEAD_DIM // 4, 1), pltpu.roll(xc, HEAD_DIM // 4, 1))
        cols.append(xc * cos + partner * sin)
    return jnp.concatenate(cols, axis=1)


def _rope_cols(xT, cosT, sinT):
    q = HEAD_DIM // 4
    heads = []
    for h in range(xT.shape[0] // HEAD_DIM):
        xh = xT[h * HEAD_DIM:(h + 1) * HEAD_DIM, :]
        partner = jnp.concatenate([xh[q:2 * q], xh[0:q], xh[3 * q:4 * q], xh[2 * q:3 * q]], axis=0)
        heads.append(xh * cosT + partner * sinT)
    return jnp.concatenate(heads, axis=0)


def _qkv_ctx_kernel(x_ref, mod_ref, wqT_ref, wkv_ref, wvT_ref, qT_ref, kv_ref, vT_ref):
    h = _modulate(x_ref, mod_ref, 3)
    qT_ref[...] = (_dot_nt(wqT_ref[...], h) * Q_SCALE).astype(bf16)
    kv_ref[...] = jnp.dot(h, wkv_ref[...], preferred_element_type=f32)
    vT_ref[...] = _dot_nt(wvT_ref[...], h).astype(bf16)


def _qkv_ctx(x, mod, wqT, wkv, wvT, layer, j):
    return pl.pallas_call(
        _qkv_ctx_kernel,
        grid=(T_CTX // TM,),
        in_specs=[
            pl.BlockSpec((TM, D), lambda i: (i, 0)),
            pl.BlockSpec((None, None, N_MOD, D), lambda i: (0, layer, 0, 0)),
            pl.BlockSpec((None, Q_DIM, D), lambda i: (j, 0, 0)),
            pl.BlockSpec((None, D, 2 * KV_DIM), lambda i: (j, 0, 0)),
            pl.BlockSpec((None, KV_DIM, D), lambda i: (j, 0, 0)),
        ],
        out_specs=[
            pl.BlockSpec((Q_DIM, TM), lambda i: (0, i)),
            pl.BlockSpec((TM, 2 * KV_DIM), lambda i: (i, 0)),
            pl.BlockSpec((KV_DIM, TM), lambda i: (0, i)),
        ],
        out_shape=[
            jax.ShapeDtypeStruct((Q_DIM, T_CTX), bf16),
            jax.ShapeDtypeStruct((T_CTX, 2 * KV_DIM), f32),
            jax.ShapeDtypeStruct((KV_DIM, T_CTX), bf16),
        ],
        compiler_params=_params("parallel"),
        name=f"qkv_ctx_l{layer}",
    )(x, mod, wqT, wkv, wvT)


def _qkv_lat_kernel(x_ref, mod_ref, wqT_ref, wkv_ref, wvT_ref, cos_ref, sin_ref, cosT_ref, sinT_ref,
                    qT_ref, k_ref, vT_ref):
    h = _modulate(x_ref, mod_ref, 3)
    qT = _rope_cols(_dot_nt(wqT_ref[...], h), cosT_ref[...], sinT_ref[...])
    qT_ref[...] = (qT * Q_SCALE).astype(bf16)
    k = jnp.dot(h, wkv_ref[:, :KV_DIM], preferred_element_type=f32)
    k_ref[...] = _rope_rows(k, cos_ref[...], sin_ref[...]).astype(bf16)
    vT_ref[...] = _dot_nt(wvT_ref[...], h).astype(bf16)


def _qkv_lat(x, mod, wqT, wkv, wvT, rope, layer, j):
    row0 = T_CTX // DEC_SEQ
    full = lambda shape: pl.BlockSpec(shape, lambda b: (0, 0))
    return pl.pallas_call(
        _qkv_lat_kernel,
        grid=(DEC_BATCH,),
        in_specs=[
            pl.BlockSpec((DEC_SEQ, D), lambda b: (row0 + b, 0)),
            pl.BlockSpec((None, None, N_MOD, D), lambda b: (1 + b, layer, 0, 0)),
            pl.BlockSpec((None, Q_DIM, D), lambda b: (j, 0, 0)),
            pl.BlockSpec((None, D, 2 * KV_DIM), lambda b: (j, 0, 0)),
            pl.BlockSpec((None, KV_DIM, D), lambda b: (j, 0, 0)),
            full((DEC_SEQ, LANES)), full((DEC_SEQ, LANES)), full((HEAD_DIM, DEC_SEQ)), full((HEAD_DIM, DEC_SEQ)),
        ],
        out_specs=[
            pl.BlockSpec((Q_DIM, DEC_SEQ), lambda b: (0, b)),
            pl.BlockSpec((DEC_SEQ, KV_DIM), lambda b: (b, 0)),
            pl.BlockSpec((KV_DIM, DEC_SEQ), lambda b: (0, b)),
        ],
        out_shape=[
            jax.ShapeDtypeStruct((Q_DIM, T_LAT), bf16),
            jax.ShapeDtypeStruct((T_LAT, KV_DIM), bf16),
            jax.ShapeDtypeStruct((KV_DIM, T_LAT), bf16),
        ],
        compiler_params=_params("parallel"),
        name=f"qkv_lat_l{layer}",
    )(x, mod, wqT, wkv, wvT, *rope)


def _softmax_t(scores, sink_row):
    m = sink_row
    for s in scores:
        m = jnp.maximum(m, jnp.max(s, axis=0, keepdims=True))
    denom = jnp.exp2(sink_row - m)
    probs = []
    for s in scores:
        p = jnp.exp2(s - m)
        denom = denom + jnp.sum(p, axis=0, keepdims=True)
        probs.append(p.astype(bf16))
    return probs, denom


def _attend(scores, values, sink_rows):
    soft = [_softmax_t(s, sink) for s, sink in zip(scores, sink_rows)]
    outs = []
    for (probs, denom), vals in zip(soft, values):
        o = None
        for p, vT in zip(probs, vals):
            pv = jnp.dot(vT, p, preferred_element_type=f32)
            o = pv if o is None else o + pv
        outs.append(o / denom)
    return outs


def _group_queries(qT_ref, g):
    heads = [qT_ref[(Q_PER_KV * g + i) * HEAD_DIM:(Q_PER_KV * g + i + 1) * HEAD_DIM, :] for i in range(Q_PER_KV)]
    return jnp.concatenate(heads, axis=1)


def _sink_row(sink_ref, g, nq):
    return jnp.concatenate([jnp.full((1, nq), sink_ref[Q_PER_KV * g + i] * LOG2E, f32) for i in range(Q_PER_KV)],
                           axis=1)


def _store_heads(o_ref, group_outs, nq):
    heads = [oT[:, i * nq:(i + 1) * nq] for oT in group_outs for i in range(Q_PER_KV)]
    o_ref[...] = jnp.concatenate(heads, axis=0).T.astype(o_ref.dtype)


def _attn_ctx_kernel(sink_ref, qT_ref, kv_ref, vT_ref, o_ref):
    k = kv_ref[:, :KV_DIM].astype(bf16)
    groups = [slice(g * HEAD_DIM, (g + 1) * HEAD_DIM) for g in range(KV_HEADS)]
    scores = [[jnp.dot(k[:, gs], _group_queries(qT_ref, g), preferred_element_type=f32)]
              for g, gs in enumerate(groups)]
    values = [[vT_ref[gs, :]] for gs in groups]
    sinks = [_sink_row(sink_ref, g, SEQ) for g in range(KV_HEADS)]
    _store_heads(o_ref, _attend(scores, values, sinks), SEQ)


def _attn_ctx(qT, kv, vT, sink):
    return pl.pallas_call(
        _attn_ctx_kernel,
        grid=(BATCH,),
        in_specs=[
            pl.BlockSpec(memory_space=pltpu.SMEM),
            pl.BlockSpec((Q_DIM, SEQ), lambda b: (0, b)),
            pl.BlockSpec((SEQ, 2 * KV_DIM), lambda b: (b, 0)),
            pl.BlockSpec((KV_DIM, SEQ), lambda b: (0, b)),
        ],
        out_specs=pl.BlockSpec((SEQ, D), lambda b: (b, 0)),
        out_shape=jax.ShapeDtypeStruct((T_CTX, D), bf16),
        compiler_params=_params("parallel"),
        name="attn_ctx",
    )(sink, qT, kv, vT)


def _attn_lat_kernel(sink_ref, qT_ref, kp_ref, kc_ref, kn_ref, vp_ref, vc_ref, vn_ref, ck_ref, cv_ref, o_ref,
                     cvT_scr):
    qb = pl.program_id(1)

    @pl.when(qb == 0)
    def _():
        cvT_scr[...] = cv_ref[...].T.astype(bf16)

    n = Q_PER_KV * BLOCK
    row = lax.broadcasted_iota(jnp.int32, (WIN_KEYS, n), 0)
    ql = lax.broadcasted_iota(jnp.int32, (WIN_KEYS, n), 1) & (BLOCK - 1)
    kj = (qb - 1) * BLOCK + row
    rel = row - ql
    valid = (rel >= 0) & (rel <= 2 * WINDOW) & (kj >= 0) & (kj < DEC_SEQ)
    k_win = jnp.concatenate([kp_ref[...], kc_ref[...], kn_ref[...]], axis=0)
    k_ctx = ck_ref[...].astype(bf16)
    groups = [slice(g * HEAD_DIM, (g + 1) * HEAD_DIM) for g in range(KV_HEADS)]
    scores, values = [], []
    for g, gs in enumerate(groups):
        qg = _group_queries(qT_ref, g)
        s_loc = jnp.where(valid, jnp.dot(k_win[:, gs], qg, preferred_element_type=f32), NEG_INF)
        s_ctx = jnp.dot(k_ctx[:, gs], qg, preferred_element_type=f32)
        scores.append([s_loc, s_ctx])
        v_win = jnp.concatenate([vp_ref[gs, :], vc_ref[gs, :], vn_ref[gs, :]], axis=1)
        values.append([v_win, cvT_scr[gs, :]])
    sinks = [_sink_row(sink_ref, g, BLOCK) for g in range(KV_HEADS)]
    _store_heads(o_ref, _attend(scores, values, sinks), BLOCK)


def _attn_lat(qT, k, vT, cache_k, cache_v, sink, j):
    nqb = DEC_SEQ // BLOCK
    prev = lambda b, i: b * nqb + jnp.maximum(i - 1, 0)
    own = lambda b, i: b * nqb + i
    nxt = lambda b, i: b * nqb + jnp.minimum(i + 1, nqb - 1)
    return pl.pallas_call(
        _attn_lat_kernel,
        grid=(DEC_BATCH, nqb),
        in_specs=[
            pl.BlockSpec(memory_space=pltpu.SMEM),
            pl.BlockSpec((Q_DIM, BLOCK), lambda b, i: (0, own(b, i))),
            pl.BlockSpec((BLOCK, KV_DIM), lambda b, i: (prev(b, i), 0)),
            pl.BlockSpec((BLOCK, KV_DIM), lambda b, i: (own(b, i), 0)),
            pl.BlockSpec((BLOCK, KV_DIM), lambda b, i: (nxt(b, i), 0)),
            pl.BlockSpec((KV_DIM, BLOCK), lambda b, i: (0, prev(b, i))),
            pl.BlockSpec((KV_DIM, BLOCK), lambda b, i: (0, own(b, i))),
            pl.BlockSpec((KV_DIM, BLOCK), lambda b, i: (0, nxt(b, i))),
            pl.BlockSpec((None, None, SEQ, KV_DIM), lambda b, i: (b, j, 0, 0)),
            pl.BlockSpec((None, None, SEQ, KV_DIM), lambda b, i: (b, j, 0, 0)),
        ],
        out_specs=pl.BlockSpec((BLOCK, D), lambda b, i: (own(b, i), 0)),
        out_shape=jax.ShapeDtypeStruct((T_LAT, D), bf16),
        scratch_shapes=[pltpu.VMEM((KV_DIM, SEQ), bf16)],
        compiler_params=_params("parallel", "arbitrary"),
        name="attn_lat",
    )(sink, qT, k, k, k, vT, vT, vT, cache_k, cache_v)


def _dft_tables(L):
    idx = np.arange(L)
    ft = np.outer(idx, idx) % (2 * L)
    cm = np.cos(np.pi * ft / L)
    sm = np.sin(np.pi * ft / L)
    nyq = np.where(idx % 2 == 0, 1.0, -1.0)
    sm_n = sm.copy()
    sm_n[0, :] = nyq
    fwd = np.concatenate([cm, sm_n], axis=0)
    wc = np.full((L,), 2.0)
    wc[0] = 1.0
    inv_c = cm.T * wc[None, :]
    inv_s = sm.T * 2.0
    inv_s[:, 0] = nyq
    inv = np.concatenate([inv_c, inv_s], axis=1) / (2 * L)
    return cm, sm, nyq, fwd, inv


def _filter_tables(L):
    t = np.arange(L, dtype=np.float64) / L
    bands = np.arange(1, FILTER_BANDS + 1, dtype=np.float64)
    ph = 2 * np.pi * t[:, None] * bands[None]
    feats = np.zeros((L, LANES))
    feats[:, :FILTER_EMB] = np.concatenate([t[:, None], np.sin(ph), np.cos(ph)], -1)
    max_decay = math.log(DECAY_TARGET) / DECAY_FAST_PCT
    min_decay = math.log(DECAY_TARGET) / DECAY_SLOW_PCT
    deltas = np.abs(np.linspace(min_decay, max_decay, D))
    decay = np.exp(-t[:, None] * deltas[None])
    return feats, decay


def _dot_hi(a, b):
    return jnp.dot(a, b, preferred_element_type=f32, precision=lax.Precision.HIGHEST)


def _split_bf16(a):
    hi = a.astype(bf16)
    return hi, (a - hi.astype(f32)).astype(bf16)


def _dot_split(t_hi, t_lo, b):
    b_hi, b_lo = _split_bf16(b)
    return (jnp.dot(t_hi, b_hi, preferred_element_type=f32) + jnp.dot(t_hi, b_lo, preferred_element_type=f32)
            + jnp.dot(t_lo, b_hi, preferred_element_type=f32))


def _filter_kernel(feat_ref, w1_ref, b1_ref, w2_ref, b2_ref, w3f_ref, w3b_ref, dec_ref, cmh_ref, cml_ref, smh_ref,
                   sml_ref, nyq_ref, o_ref, a_scr):
    @pl.when((pl.program_id(0) == 0) & (pl.program_id(1) == 0))
    def _():
        a1 = jnp.sin(_dot_hi(feat_ref[...], w1_ref[...]) + b1_ref[...])
        a_scr[...] = jnp.sin(_dot_hi(a1, w2_ref[...]) + b2_ref[...])

    a = a_scr[...]
    dec = dec_ref[...]
    fwd = _dot_hi(a, w3f_ref[...]) * dec
    bwd = _dot_hi(a, w3b_ref[...]) * dec
    row = lax.broadcasted_iota(jnp.int32, fwd.shape, 0)
    bwd = jnp.where(row == 0, 0.0, bwd)
    norm = jnp.sum(jnp.abs(fwd), axis=0, keepdims=True) + jnp.sum(jnp.abs(bwd), axis=0, keepdims=True) + 1e-6
    even = (fwd + bwd) / norm
    odd = (bwd - fwd) / norm
    hr = _dot_split(cmh_ref[...], cml_ref[...], even)
    hi = _dot_split(smh_ref[...], sml_ref[...], odd)
    h_nyq = jnp.sum(even * nyq_ref[...], axis=0, keepdims=True)
    o_ref[0] = hr
    o_ref[1] = jnp.where(row == 0, 0.0, hi)
    o_ref[2] = jnp.where(row == 0, h_nyq, hr)


def _hyena_filter(L, j, w1p, b1p, w2p, b2p, w3p, name):
    cm, sm, nyq, _, _ = _dft_tables(L)
    feats, decay = _filter_tables(L)
    td = 512
    nd = D // td
    full = lambda shape: pl.BlockSpec(shape, lambda o, d: (0,) * len(shape))
    return pl.pallas_call(
        _filter_kernel,
        grid=(2, nd),
        in_specs=[
            full((L, LANES)),
            pl.BlockSpec((None, LANES, LANES), lambda o, d: (j, 0, 0)),
            pl.BlockSpec((None, 1, LANES), lambda o, d: (j, 0, 0)),
            pl.BlockSpec((None, LANES, LANES), lambda o, d: (j, 0, 0)),
            pl.BlockSpec((None, 1, LANES), lambda o, d: (j, 0, 0)),
            pl.BlockSpec((None, LANES, td), lambda o, d: (j, 0, (2 * o) * nd + d)),
            pl.BlockSpec((None, LANES, td), lambda o, d: (j, 0, (2 * o + 1) * nd + d)),
            pl.BlockSpec((L, td), lambda o, d: (0, d)),
            full((L, L)), full((L, L)), full((L, L)), full((L, L)),
            full((L, 1)),
        ],
        out_specs=pl.BlockSpec((3, L, td), lambda o, d: (o, 0, d)),
        out_shape=jax.ShapeDtypeStruct((6, L, D), f32),
        scratch_shapes=[pltpu.VMEM((L, LANES), f32)],
        compiler_params=_params("arbitrary", "arbitrary"),
        name=name,
    )(jnp.asarray(feats, f32), w1p, b1p, w2p, b2p, w3p, w3p, jnp.asarray(decay, f32),
      *_np_split_bf16(cm), *_np_split_bf16(sm), jnp.asarray(nyq[:, None], f32))


def _np_split_bf16(t):
    hi = t.astype(bf16)
    lo = (t - hi.astype(np.float64)).astype(bf16)
    return jnp.asarray(hi), jnp.asarray(lo)


def _short_conv(u, w, b):
    L = u.shape[0]
    row = lax.broadcasted_iota(jnp.int32, u.shape, 0)
    prev = jnp.where(row == 0, 0.0, pltpu.roll(u, 1, 0))
    nxt = jnp.where(row == L - 1, 0.0, pltpu.roll(u, L - 1, 0))
    return prev * w[0:1, :] + u * w[1:2, :] + nxt * w[2:3, :] + b


def _spectral_product(zs, h_ref, o):
    L = zs.shape[0] // 2
    zr, zi = zs[:L], zs[L:]
    hr, hi_m, hr_n = h_ref[3 * o], h_ref[3 * o + 1], h_ref[3 * o + 2]
    return jnp.concatenate([zr * hr + zi * hi_m, zi * hr_n - zr * hi_m], axis=0).astype(bf16)


def _hyena_conv_kernel(pv_ref, p1_ref, p2_ref, wv_ref, w1_ref, w2_ref, bv_ref, b1_ref, b2_ref, h_ref, d_ref,
                       fwd_ref, inv_ref, o_ref):
    L = h_ref.shape[1]
    seqs = [slice(s * L, (s + 1) * L) for s in range(pv_ref.shape[0] // L)]
    dft = lambda z: jnp.dot(fwd_ref[...], z.astype(bf16), preferred_element_type=f32)
    idft = lambda y: jnp.dot(inv_ref[...], y, preferred_element_type=f32)
    z = [_short_conv(pv_ref[r, :].astype(f32), wv_ref[...], bv_ref[...]) for r in seqs]
    gate_in = [(p1_ref, w1_ref, b1_ref), (p2_ref, w2_ref, b2_ref)]
    for o in range(2):
        zs = [dft(zz) for zz in z]
        ys = [_spectral_product(s, h_ref, o) for s in zs]
        y = [idft(s) for s in ys]
        p_ref, w_ref, b_ref = gate_in[o]
        gates = [_short_conv(p_ref[r, :].astype(f32), w_ref[...], b_ref[...]) for r in seqs]
        z = [g * (yy + zz * d_ref[o:o + 1, :]) for g, yy, zz in zip(gates, y, z)]
    for r, zz in zip(seqs, z):
        o_ref[r, :] = zz.astype(bf16)


def _hyena_conv(p, conv_w, conv_b, spec, hy_d, j, L, nb, nseq, row0, td, name):
    _, _, _, fwd, inv = _dft_tables(L)
    nd = D // td
    once = pl.Buffered(1)
    in_specs = []
    for c in range(3):
        in_specs.append(pl.BlockSpec((nseq * L, td), lambda d, b, c=c: (row0 // nseq + b, c * nd + d)))
    for c in range(3):
        in_specs.append(pl.BlockSpec((None, 3, td), lambda d, b, c=c: (j, 0, c * nd + d)))
    for c in range(3):
        in_specs.append(pl.BlockSpec((None, 1, td), lambda d, b, c=c: (j, 0, c * nd + d)))
    in_specs += [
        pl.BlockSpec((6, L, td), lambda d, b: (0, 0, d), pipeline_mode=once),
        pl.BlockSpec((None, 2, td), lambda d, b: (j, 0, d)),
        pl.BlockSpec((2 * L, L), lambda d, b: (0, 0), pipeline_mode=once),
        pl.BlockSpec((L, 2 * L), lambda d, b: (0, 0), pipeline_mode=once),
    ]
    return pl.pallas_call(
        _hyena_conv_kernel,
        grid=(nd, nb // nseq),
        in_specs=in_specs,
        out_specs=pl.BlockSpec((nseq * L, td), lambda d, b: (b, d)),
        out_shape=jax.ShapeDtypeStruct((nb * L, D), bf16),
        compiler_params=_params("parallel", "parallel"),
        name=name,
    )(p, p, p, conv_w, conv_w, conv_w, conv_b, conv_b, conv_b, spec, hy_d,
      jnp.asarray(fwd, f32).astype(bf16), jnp.asarray(inv, f32).astype(bf16))


def _pad2(a, rows, cols):
    return jnp.pad(a, [(0, 0)] * (a.ndim - 2) + [(0, rows - a.shape[-2]), (0, cols - a.shape[-1])])


def kernel(x_prompt, x_sample, cache_k, cache_v, c, c_ctx, ada_w, ada_b, ln_g, ln_b, ffn_w1, ffn_w2, attn_w_qkv,
           attn_w_o, attn_sink, hy_w_in, hy_conv_w, hy_conv_b, hy_f_w1, hy_f_b1, hy_f_w2, hy_f_b2, hy_f_w3, hy_d,
           hy_w_out):
    cond =jnp.concatenate([c_ctx[None], c, jnp.zeros((N_COND - 1 - DEC_BATCH, D), f32)], axis=0)
    mod = _ada(cond, ada_w, ada_b)

    w1 = ffn_w1.astype(bf16)
    w2 = ffn_w2.astype(bf16)
    w_qkv = attn_w_qkv.astype(bf16)
    wqT = jnp.swapaxes(w_qkv[:, :, :Q_DIM], 1, 2)
    wkv = w_qkv[:, :, Q_DIM:]
    wvT = jnp.swapaxes(w_qkv[:, :, Q_DIM + KV_DIM:], 1, 2)
    w_o = attn_w_o.astype(bf16)
    w_in = hy_w_in.astype(bf16)
    w_out = hy_w_out.astype(bf16)
    ck = cache_k.reshape(DEC_BATCH, -1, SEQ, KV_DIM)
    cv = cache_v.reshape(DEC_BATCH, -1, SEQ, KV_DIM)
    rope = _rope_tables()
    fw1 = _pad2(hy_f_w1, LANES, LANES)
    fb1 = _pad2(hy_f_b1[:, None, :], 1, LANES)
    fw2 = _pad2(hy_f_w2, LANES, LANES)
    fb2 = _pad2(hy_f_b2[:, None, :], 1, LANES)
    fw3 = _pad2(hy_f_w3, LANES, hy_f_w3.shape[-1])
    conv_b = hy_conv_b[:, None, :]
    ln_g = ln_g.reshape(DEPTH * 3, 1, D)
    ln_b = ln_b.reshape(DEPTH * 3, 1, D)

    new_k, new_v = [], []
    xs = (x_prompt.reshape(T_CTX, D), x_sample.reshape(T_LAT, D))
    for i in range(DEPTH):
        j = i // 2
        x = _ffn(xs, mod, w1, w2, ln_g, ln_b, i, 0)
        if i % 2 == 0:
            qT_c, kv_c, vT_c = _qkv_ctx(x, mod, wqT, wkv, wvT, i, j)
            qT_l, k_l, vT_l = _qkv_lat(x, mod, wqT, wkv, wvT, rope, i, j)
            new_k.append(kv_c[:, :KV_DIM].reshape(BATCH, SEQ, KV_HEADS, HEAD_DIM))
            new_v.append(kv_c[:, KV_DIM:].reshape(BATCH, SEQ, KV_HEADS, HEAD_DIM))
            sink = attn_sink[j]
            mixer = (_attn_ctx(qT_c, kv_c, vT_c, sink), _attn_lat(qT_l, k_l, vT_l, ck, cv, sink, j), w_o, j)
        else:
            p = _modmm(x, mod, w_in, i, j, f"hy_in_l{i}")
            spec_ctx = _hyena_filter(SEQ, j, fw1, fb1, fw2, fb2, fw3, f"hy_filter_ctx_l{i}")
            spec_lat = _hyena_filter(DEC_SEQ, j, fw1, fb1, fw2, fb2, fw3, f"hy_filter_lat_l{i}")
            z_c = _hyena_conv(p, hy_conv_w, conv_b, spec_ctx, hy_d, j, SEQ, BATCH, 4, 0, 512, f"hy_conv_ctx_l{i}")
            z_l = _hyena_conv(p, hy_conv_w, conv_b, spec_lat, hy_d, j, DEC_SEQ, DEC_BATCH, 2, T_CTX // DEC_SEQ, 256,
                              f"hy_conv_lat_l{i}")
            mixer = (z_c, z_l, w_out, j)
        if i < DEPTH - 1:
            xs = (_ffn((x,), mod, w1, w2, ln_g, ln_b, i, 1, mixer=mixer),)
    y_prompt = _ffn((x,), mod, w1, w2, ln_g, ln_b, DEPTH - 1, 1, mixer=mixer, tiles=(0, N_CTX_TILES),
                    name="ffn_last_ctx")
    y_sample = _ffn((x,), mod, w1, w2, ln_g, ln_b, DEPTH - 1, 1, mixer=mixer, tiles=(N_CTX_TILES, N_TILES),
                    name="ffn_last_lat")
    return (y_prompt.reshape(BATCH, SEQ, D), y_sample.reshape(DEC_BATCH, DEC_SEQ, D),
            jnp.stack(new_k, axis=1), jnp.stack(new_v, axis=1))
```
